```python
import math
import jax
import jax.numpy as jnp
from jax import lax
import numpy as np

D_MODEL = 1024
BATCH = 16
SEQ = 256
DEPTH = 4
DEC_BATCH = 8
DEC_SEQ = 4096
PAST_LEN = 512

GRID_W = 64
N_MIXERS = 4
EPS = 1e-6
ROPE_BASE = 10000.0
Q_BLOCK = 128

WA_HEADS = 16
WA_KV_HEADS = 4
WA_HEAD_DIM = D_MODEL // WA_HEADS
WA_GROUP = WA_HEADS // WA_KV_HEADS
WINDOW = 128

GLA_HEADS = 4
GLA_DK = D_MODEL // 2 // GLA_HEADS
GLA_DV = D_MODEL // GLA_HEADS
GLA_RANK = 16
GLA_TAU = 16.0
GLA_CHUNK = 64

DA_HEADS = 8
DA_HEAD_DIM = D_MODEL // DA_HEADS // 2

SSD_D_INNER = 2 * D_MODEL
SSD_HEAD_DIM = 64
SSD_HEADS = SSD_D_INNER // SSD_HEAD_DIM
SSD_GROUPS = 4
SSD_STATE = 128
SSD_CONV = 3
SSD_CHUNK = 128

D_FF = 256 * ((8 * D_MODEL // 3 + 255) // 256)
FFN_CONV = 3

kernel_name = 'hybrid_prefix_diffusion_step'


def rms_norm(x, g):
    xf = x.astype(jnp.float32)
    y = xf * lax.rsqrt(jnp.mean(xf * xf, axis=-1, keepdims=True) + EPS)
    return (y * g.astype(jnp.float32)).astype(x.dtype)


def modulate(h, shift, scale):
    return h * (1.0 + scale) + shift


def ada_modulation(cond, w, b):
    m = jax.nn.silu(cond) @ w + b
    m = m.reshape(cond.shape[:-1] + (6, 1, D_MODEL))
    return [m[..., k, :, :] for k in range(6)]


def _to_chunks(a, size):
    b, l = a.shape[:2]
    return jnp.moveaxis(a.reshape((b, l // size, size) + a.shape[2:]), 1, 0)


def _from_chunks(a):
    a = jnp.moveaxis(a, 0, 1)
    return a.reshape((a.shape[0], a.shape[1] * a.shape[2]) + a.shape[3:])


def dwconv_centered(x, w, b):
    k, ch = w.shape
    y = lax.conv_general_dilated(x, w.reshape(k, 1, ch).astype(x.dtype), window_strides=(1,),
                                 padding=[(k // 2, k // 2)], dimension_numbers=('NWC', 'WIO', 'NWC'),
                                 feature_group_count=ch)
    return y + b.astype(x.dtype)


def axial_rope_tables(n_tokens, dim):
    rows = n_tokens // GRID_W
    row = jnp.repeat(jnp.arange(rows, dtype=jnp.float32), GRID_W)
    col = jnp.tile(jnp.arange(GRID_W, dtype=jnp.float32), rows)
    axis_dim = dim // 2
    inv = ROPE_BASE ** (-jnp.arange(0, axis_dim, 2, dtype=jnp.float32) / axis_dim)
    ar = row[:, None] * inv
    ac = col[:, None] * inv
    return jnp.cos(ar), jnp.sin(ar), jnp.cos(ac), jnp.sin(ac)


def _rope_half(x, cos, sin):
    x1, x2 = jnp.split(x, 2, axis=-1)
    return jnp.concatenate([x1 * cos - x2 * sin, x2 * cos + x1 * sin], axis=-1)


def apply_axial_rope(x, tabs):
    cos_r, sin_r, cos_c, sin_c = tabs
    shape = (x.shape[1],) + (1,) * (x.ndim - 3) + (cos_r.shape[-1],)
    f = lambda a: a.reshape(shape).astype(x.dtype)
    xr, xc = jnp.split(x, 2, axis=-1)
    return jnp.concatenate([_rope_half(xr, f(cos_r), f(sin_r)), _rope_half(xc, f(cos_c), f(sin_c))], axis=-1)


def window_attn_qkv(h, w_qkv):
    b, l, _ = h.shape
    nq = WA_HEADS * WA_HEAD_DIM
    nkv = WA_KV_HEADS * WA_HEAD_DIM
    q, k, v = jnp.split(h @ w_qkv, [nq, nq + nkv], axis=-1)
    return (q.reshape(b, l, WA_KV_HEADS, WA_GROUP, WA_HEAD_DIM),
            k.reshape(b, l, WA_KV_HEADS, WA_HEAD_DIM),
            v.reshape(b, l, WA_KV_HEADS, WA_HEAD_DIM))


def gqa_sink_block(qb, kk, vv, mask, sink):
    s = jnp.einsum('bqhgd,bkhd->bhgqk', qb, kk, preferred_element_type=jnp.float32) * WA_HEAD_DIM ** -0.5
    if mask is not None:
        s = jnp.where(mask, s, -jnp.inf)
    sk = jnp.broadcast_to(sink.astype(jnp.float32).reshape(1, WA_KV_HEADS, WA_GROUP, 1, 1), s.shape[:-1] + (1,))
    p = jax.nn.softmax(jnp.concatenate([s, sk], axis=-1), axis=-1)[..., :-1]
    o = jnp.einsum('bhgqk,bkhd->bqhgd', p.astype(vv.dtype), vv)
    return o.reshape(qb.shape[0], qb.shape[1], WA_HEADS * WA_HEAD_DIM)


def window_attn_context(h, w_qkv, w_o, sink):
    q, k, v = window_attn_qkv(h, w_qkv)
    def block(i):
        qb = lax.dynamic_slice_in_dim(q, i * Q_BLOCK, Q_BLOCK, axis=1)
        return gqa_sink_block(qb, k, v, None, sink)
    o = _from_chunks(lax.map(block, jnp.arange(h.shape[1] // Q_BLOCK)))
    return o @ w_o, k, v


def window_attn_latent(h, k_ctx, v_ctx, w_qkv, w_o, sink):
    q, k, v = window_attn_qkv(h, w_qkv)
    l = h.shape[1]
    tabs = axial_rope_tables(l, WA_HEAD_DIM)
    q = apply_axial_rope(q, tabs)
    k = apply_axial_rope(k, tabs)
    pad = ((0, 0), (Q_BLOCK, Q_BLOCK), (0, 0), (0, 0))
    kp = jnp.pad(k, pad)
    vp = jnp.pad(v, pad)
    n_ctx = k_ctx.shape[1]
    k_ctx = k_ctx.astype(k.dtype)
    v_ctx = v_ctx.astype(v.dtype)
    diff = jnp.arange(Q_BLOCK)[:, None] - jnp.arange(3 * Q_BLOCK)[None, :] + Q_BLOCK
    band = jnp.abs(diff) <= WINDOW
    ctx_mask = jnp.ones((Q_BLOCK, n_ctx), dtype=bool)
    def block(i):
        qb = lax.dynamic_slice_in_dim(q, i * Q_BLOCK, Q_BLOCK, axis=1)
        kb = lax.dynamic_slice_in_dim(kp, i * Q_BLOCK, 3 * Q_BLOCK, axis=1)
        vb = lax.dynamic_slice_in_dim(vp, i * Q_BLOCK, 3 * Q_BLOCK, axis=1)
        kpos = (i - 1) * Q_BLOCK + jnp.arange(3 * Q_BLOCK)
        lat_mask = band & ((kpos >= 0) & (kpos < l))[None, :]
        mask = jnp.concatenate([lat_mask, ctx_mask], axis=1)
        kk = jnp.concatenate([kb, k_ctx], axis=1)
        vv = jnp.concatenate([vb, v_ctx], axis=1)
        return gqa_sink_block(qb, kk, vv, mask, sink)
    o = _from_chunks(lax.map(block, jnp.arange(l // Q_BLOCK)))
    return o @ w_o


def gla_scan(q, k, v, logg, s0):
    f32 = jnp.float32
    causal = jnp.tril(jnp.ones((GLA_CHUNK, GLA_CHUNK), dtype=bool))
    def step(s, inp):
        qc, kc, vc, gc = inp
        bcum = jnp.cumsum(gc, axis=1)
        blast = bcum[:, -1]
        qd = qc * jnp.exp(bcum)
        kd = kc * jnp.exp(-bcum)
        att = jnp.where(causal, jnp.einsum('bihd,bjhd->bhij', qd, kd), 0.0)
        o = jnp.einsum('bhij,bjhv->bihv', att, vc) + jnp.einsum('bihd,bhdv->bihv', qd, s)
        kr = kc * jnp.exp(blast[:, None] - bcum)
        s = s * jnp.exp(blast)[..., None] + jnp.einsum('bjhd,bjhv->bhdv', kr, vc)
        return s, o
    xs = tuple(_to_chunks(a.astype(f32), GLA_CHUNK) for a in (q, k, v, logg))
    s, o = lax.scan(step, s0.astype(f32), xs)
    return _from_chunks(o), s


def gla_mixer(h, s0_f, s0_b, w_qkvr, w_gf1, w_gf2, b_gf, w_gb1, w_gb2, b_gb, g_head, w_o):
    b, l, _ = h.shape
    nk = GLA_HEADS * GLA_DK
    nv = GLA_HEADS * GLA_DV
    q, k, v, r = jnp.split(h @ w_qkvr, [nk, 2 * nk, 2 * nk + nv], axis=-1)
    q = q.reshape(b, l, GLA_HEADS, GLA_DK) * GLA_DK ** -0.5
    k = k.reshape(b, l, GLA_HEADS, GLA_DK)
    v = v.reshape(b, l, GLA_HEADS, GLA_DV)
    def log_gate(w1, w2, bias):
        z = ((h @ w1) @ w2 + bias).astype(jnp.float32)
        return (jax.nn.log_sigmoid(z) / GLA_TAU).reshape(b, l, GLA_HEADS, GLA_DK)
    flip = lambda a: jnp.flip(a, axis=1)
    o_f, s_f = gla_scan(q, k, v, log_gate(w_gf1, w_gf2, b_gf), s0_f)
    o_b, s_b = gla_scan(flip(q), flip(k), flip(v), flip(log_gate(w_gb1, w_gb2, b_gb)), s0_b)
    o = rms_norm(o_f + flip(o_b), g_head).reshape(b, l, nv) * jax.nn.silu(r.astype(jnp.float32))
    return o.astype(h.dtype) @ w_o, s_f, s_b


def diff_qkv(h, w_qkv):
    b, l, _ = h.shape
    q, k, v = jnp.split(h @ w_qkv, 3, axis=-1)
    return (q.reshape(b, l, DA_HEADS, 2, DA_HEAD_DIM),
            k.reshape(b, l, DA_HEADS, 2, DA_HEAD_DIM),
            v.reshape(b, l, DA_HEADS, 2 * DA_HEAD_DIM))


def diff_lambda(lq1, lk1, lq2, lk2, lam_init):
    f = lambda a: a.astype(jnp.float32)
    return jnp.exp(jnp.sum(f(lq1) * f(lk1))) - jnp.exp(jnp.sum(f(lq2) * f(lk2))) + lam_init


def diff_sweep(q, kk, vv, lam):
    def block(i):
        qb = lax.dynamic_slice_in_dim(q, i * Q_BLOCK, Q_BLOCK, axis=1)
        s = jnp.einsum('bqhcd,bkhcd->bhcqk', qb, kk, preferred_element_type=jnp.float32) * DA_HEAD_DIM ** -0.5
        p = jax.nn.softmax(s, axis=-1)
        a = p[:, :, 0] - lam * p[:, :, 1]
        return jnp.einsum('bhqk,bkhe->bqhe', a.astype(vv.dtype), vv)
    return _from_chunks(lax.map(block, jnp.arange(q.shape[1] // Q_BLOCK)))


def diff_out(o, g_sub, lam_init, w_o):
    b, l = o.shape[:2]
    o = rms_norm(o, g_sub) * (1.0 - lam_init)
    return o.reshape(b, l, DA_HEADS * 2 * DA_HEAD_DIM) @ w_o


def diff_attn_context(h, w_qkv, lam, lam_init, g_sub, w_o):
    q, k, v = diff_qkv(h, w_qkv)
    return diff_out(diff_sweep(q, k, v, lam), g_sub, lam_init, w_o), k, v


def diff_attn_latent(h, k_ctx, v_ctx, w_qkv, lam, lam_init, g_sub, w_o):
    q, k, v = diff_qkv(h, w_qkv)
    tabs = axial_rope_tables(h.shape[1], DA_HEAD_DIM)
    q = apply_axial_rope(q, tabs)
    k = apply_axial_rope(k, tabs)
    kk = jnp.concatenate([k, k_ctx.astype(k.dtype)], axis=1)
    vv = jnp.concatenate([v, v_ctx.astype(v.dtype)], axis=1)
    return diff_out(diff_sweep(q, kk, vv, lam), g_sub, lam_init, w_o)


def ssd_scan(x, dt, a, bm, cm, s0):
    f32 = jnp.float32
    b, l = x.shape[:2]
    r = SSD_HEADS // SSD_GROUPS
    xg = x.astype(f32).reshape(b, l, SSD_GROUPS, r, SSD_HEAD_DIM)
    dtg = dt.astype(f32).reshape(b, l, SSD_GROUPS, r)
    lag = dtg * a.astype(f32).reshape(SSD_GROUPS, r)
    causal = jnp.tril(jnp.ones((SSD_CHUNK, SSD_CHUNK), dtype=bool))
    def step(s, inp):
        xc, dtc, lac, bc, cc = inp
        cum = jnp.cumsum(lac, axis=1)
        cum_t = jnp.moveaxis(cum, 1, -1)
        seg = cum_t[..., :, None] - cum_t[..., None, :]
        decay = jnp.exp(jnp.where(causal, seg, -jnp.inf))
        cb = jnp.einsum('bign,bjgn->bgij', cc, bc)
        w = cb[:, :, None] * decay * jnp.moveaxis(dtc, 1, -1)[..., None, :]
        y = jnp.einsum('bgrij,bjgrp->bigrp', w, xc)
        y = y + jnp.einsum('bign,bgrpn->bigrp', cc, s) * jnp.exp(cum)[..., None]
        to_end = jnp.exp(cum[:, -1:] - cum) * dtc
        s = s * jnp.exp(cum[:, -1])[..., None, None] + jnp.einsum('bjgr,bjgrp,bjgn->bgrpn', to_end, xc, bc)
        return s, y
    xs = tuple(_to_chunks(t, SSD_CHUNK) for t in (xg, dtg, lag, bm.astype(f32), cm.astype(f32)))
    s, y = lax.scan(step, s0.astype(f32).reshape(b, SSD_GROUPS, r, SSD_HEAD_DIM, SSD_STATE), xs)
    return (_from_chunks(y).reshape(b, l, SSD_HEADS, SSD_HEAD_DIM),
            s.reshape(b, SSD_HEADS, SSD_HEAD_DIM, SSD_STATE))


def ssd_mixer(h, s0_f, s0_b, w_in, conv_w, conv_b, a_log_f, a_log_b, dt_bias_f, dt_bias_b, d_skip, g_norm, w_out):
    b, l, _ = h.shape
    gn = SSD_GROUPS * SSD_STATE
    z, xbc, dt = jnp.split(h @ w_in, [SSD_D_INNER, 2 * SSD_D_INNER + 2 * gn], axis=-1)
    xbc = jax.nn.silu(dwconv_centered(xbc, conv_w, conv_b))
    x, bm, cm = jnp.split(xbc, [SSD_D_INNER, SSD_D_INNER + gn], axis=-1)
    x = x.reshape(b, l, SSD_HEADS, SSD_HEAD_DIM)
    bm = bm.reshape(b, l, SSD_GROUPS, SSD_STATE)
    cm = cm.reshape(b, l, SSD_GROUPS, SSD_STATE)
    dt = dt.astype(jnp.float32)
    dt_f = jax.nn.softplus(dt[..., :SSD_HEADS] + dt_bias_f.astype(jnp.float32))
    dt_b = jax.nn.softplus(dt[..., SSD_HEADS:] + dt_bias_b.astype(jnp.float32))
    flip = lambda a: jnp.flip(a, axis=1)
    y_f, s_f = ssd_scan(x, dt_f, -jnp.exp(a_log_f.astype(jnp.float32)), bm, cm, s0_f)
    y_b, s_b = ssd_scan(flip(x), flip(dt_b), -jnp.exp(a_log_b.astype(jnp.float32)), flip(bm), flip(cm), s0_b)
    y = y_f + flip(y_b) + x.astype(jnp.float32) * d_skip.astype(jnp.float32)[:, None]
    y = y.reshape(b, l, SSD_D_INNER) * jax.nn.silu(z.astype(jnp.float32))
    y = rms_norm(y, g_norm)
    return y.astype(h.dtype) @ w_out, s_f, s_b


def conv_ffn(h, w_up, conv_w, conv_b, w_down):
    u = dwconv_centered(h @ w_up, conv_w, conv_b)
    g, v = jnp.split(u, 2, axis=-1)
    return (jax.nn.silu(g) * v) @ w_down


def setup_inputs(seed: int = 0) -> dict:
    key = jax.random.key(seed)
    ks = jax.random.split(key, 64)
    counter = [0]
    f32 = jnp.float32
    def nxt():
        counter[0] += 1
        return ks[counter[0] - 1]
    def nrm(shape, scale=1.0):
        return jax.random.normal(nxt(), shape, f32) * scale
    def lin(shape):
        return nrm(shape, shape[-2] ** -0.5)
    def gain(shape):
        return 1.0 + nrm(shape, 0.02)
    na, nb, nc, nd = [len(range(m, DEPTH, N_MIXERS)) for m in range(N_MIXERS)]
    nk = GLA_HEADS * GLA_DK
    nv = GLA_HEADS * GLA_DV
    gn = SSD_GROUPS * SSD_STATE
    dt0 = jnp.exp(jax.random.uniform(nxt(), (nd, 2, SSD_HEADS), f32, math.log(1e-3), math.log(1e-1)))
    dt_bias = dt0 + jnp.log(-jnp.expm1(-dt0))
    a_log = jnp.log(jax.random.uniform(nxt(), (nd, 2, SSD_HEADS), f32, 1.0, 16.0))
    return {
        'x_prompt': nrm((BATCH, SEQ, D_MODEL)),
        'x_sample': nrm((DEC_BATCH, DEC_SEQ, D_MODEL)),
        'c': nrm((DEC_BATCH, D_MODEL)),
        'c_ctx': nrm((D_MODEL,)),
        'cache_win_k': nrm((DEC_BATCH, na, PAST_LEN, WA_KV_HEADS, WA_HEAD_DIM)),
        'cache_win_v': nrm((DEC_BATCH, na, PAST_LEN, WA_KV_HEADS, WA_HEAD_DIM)),
        'state_gla_fwd': nrm((DEC_BATCH, nb, GLA_HEADS, GLA_DK, GLA_DV), 0.1),
        'state_gla_bwd': nrm((DEC_BATCH, nb, GLA_HEADS, GLA_DK, GLA_DV), 0.1),
        'cache_diff_k': nrm((DEC_BATCH, nc, PAST_LEN, DA_HEADS, 2, DA_HEAD_DIM)),
        'cache_diff_v': nrm((DEC_BATCH, nc, PAST_LEN, DA_HEADS, 2 * DA_HEAD_DIM)),
        'state_ssd_fwd': nrm((DEC_BATCH, nd, SSD_HEADS, SSD_HEAD_DIM, SSD_STATE), 0.1),
        'state_ssd_bwd': nrm((DEC_BATCH, nd, SSD_HEADS, SSD_HEAD_DIM, SSD_STATE), 0.1),
        'ada_w': lin((DEPTH, D_MODEL, 6 * D_MODEL)),
        'ada_b': nrm((DEPTH, 6 * D_MODEL), 0.02),
        'norm_mix': gain((DEPTH, D_MODEL)),
        'norm_ffn': gain((DEPTH, D_MODEL)),
        'ffn_w_up': lin((DEPTH, D_MODEL, 2 * D_FF)),
        'ffn_conv_w': nrm((DEPTH, FFN_CONV, 2 * D_FF), FFN_CONV ** -0.5),
        'ffn_conv_b': nrm((DEPTH, 2 * D_FF), 0.01),
        'ffn_w_down': lin((DEPTH, D_FF, D_MODEL)),
        'final_norm': gain((D_MODEL,)),
        'win_w_qkv': lin((na, D_MODEL, (WA_HEADS + 2 * WA_KV_HEADS) * WA_HEAD_DIM)),
        'win_w_o': lin((na, WA_HEADS * WA_HEAD_DIM, D_MODEL)),
        'win_sink': nrm((na, WA_HEADS)),
        'gla_w_qkvr': lin((nb, D_MODEL, 2 * nk + 2 * nv)),
        'gla_w_gf1': lin((nb, D_MODEL, GLA_RANK)),
        'gla_w_gf2': lin((nb, GLA_RANK, nk)),
        'gla_b_gf': nrm((nb, nk), 0.01),
        'gla_w_gb1': lin((nb, D_MODEL, GLA_RANK)),
        'gla_w_gb2': lin((nb, GLA_RANK, nk)),
        'gla_b_gb': nrm((nb, nk), 0.01),
        'gla_norm': gain((nb, GLA_DV)),
        'gla_w_o': lin((nb, nv, D_MODEL)),
        'diff_w_qkv': lin((nc, D_MODEL, 3 * DA_HEADS * 2 * DA_HEAD_DIM)),
        'diff_lq1': nrm((nc, DA_HEAD_DIM), 0.1),
        'diff_lk1': nrm((nc, DA_HEAD_DIM), 0.1),
        'diff_lq2': nrm((nc, DA_HEAD_DIM), 0.1),
        'diff_lk2': nrm((nc, DA_HEAD_DIM), 0.1),
        'diff_norm': gain((nc, 2 * DA_HEAD_DIM)),
        'diff_w_o': lin((nc, DA_HEADS * 2 * DA_HEAD_DIM, D_MODEL)),
        'ssd_w_in': lin((nd, D_MODEL, 2 * SSD_D_INNER + 2 * gn + 2 * SSD_HEADS)),
        'ssd_conv_w': nrm((nd, SSD_CONV, SSD_D_INNER + 2 * gn), SSD_CONV ** -0.5),
        'ssd_conv_b': nrm((nd, SSD_D_INNER + 2 * gn), 0.01),
        'ssd_a_log_f': a_log[:, 0],
        'ssd_a_log_b': a_log[:, 1],
        'ssd_dt_bias_f': dt_bias[:, 0],
        'ssd_dt_bias_b': dt_bias[:, 1],
        'ssd_d': 1.0 + nrm((nd, SSD_HEADS), 0.1),
        'ssd_norm': gain((nd, SSD_D_INNER)),
        'ssd_w_out': lin((nd, SSD_D_INNER, D_MODEL)),
    }


def reference(x_prompt, x_sample, c, c_ctx, cache_win_k, cache_win_v, state_gla_fwd, state_gla_bwd,
              cache_diff_k, cache_diff_v, state_ssd_fwd, state_ssd_bwd,
              ada_w, ada_b, norm_mix, norm_ffn, ffn_w_up, ffn_conv_w, ffn_conv_b, ffn_w_down, final_norm,
              win_w_qkv, win_w_o, win_sink,
              gla_w_qkvr, gla_w_gf1, gla_w_gf2, gla_b_gf, gla_w_gb1, gla_w_gb2, gla_b_gb, gla_norm, gla_w_o,
              diff_w_qkv, diff_lq1, diff_lk1, diff_lq2, diff_lk2, diff_norm, diff_w_o,
              ssd_w_in, ssd_conv_w, ssd_conv_b, ssd_a_log_f, ssd_a_log_b, ssd_dt_bias_f, ssd_dt_bias_b,
              ssd_d, ssd_norm, ssd_w_out):
    xp, xs = x_prompt, x_sample
    bp = xp.shape[0]
    new_win_k, new_win_v, new_gla_f, new_gla_b = [], [], [], []
    new_diff_k, new_diff_v, new_ssd_f, new_ssd_b = [], [], [], []
    for i in range(DEPTH):
        kind, j = i % N_MIXERS, i // N_MIXERS
        mod_p = ada_modulation(c_ctx, ada_w[i], ada_b[i])
        mod_s = ada_modulation(c, ada_w[i], ada_b[i])
        hp = modulate(rms_norm(xp, norm_mix[i]), mod_p[0], mod_p[1])
        hs = modulate(rms_norm(xs, norm_mix[i]), mod_s[0], mod_s[1])
        if kind == 0:
            op, k_c, v_c = window_attn_context(hp, win_w_qkv[j], win_w_o[j], win_sink[j])
            os_ = window_attn_latent(hs, cache_win_k[:, j], cache_win_v[:, j], win_w_qkv[j], win_w_o[j], win_sink[j])
            new_win_k.append(k_c)
            new_win_v.append(v_c)
        elif kind == 1:
            gla_w = (gla_w_qkvr[j], gla_w_gf1[j], gla_w_gf2[j], gla_b_gf[j], gla_w_gb1[j], gla_w_gb2[j],
                     gla_b_gb[j], gla_norm[j], gla_w_o[j])
            zeros = jnp.zeros((bp, GLA_HEADS, GLA_DK, GLA_DV), jnp.float32)
            op, s_f, s_b = gla_mixer(hp, zeros, zeros, *gla_w)
            os_, _, _ = gla_mixer(hs, state_gla_fwd[:, j], state_gla_bwd[:, j], *gla_w)
            new_gla_f.append(s_f)
            new_gla_b.append(s_b)
        elif kind == 2:
            lam_init = 0.8 - 0.6 * math.exp(-0.3 * i)
            lam = diff_lambda(diff_lq1[j], diff_lk1[j], diff_lq2[j], diff_lk2[j], lam_init)
            op, k_c, v_c = diff_attn_context(hp, diff_w_qkv[j], lam, lam_init, diff_norm[j], diff_w_o[j])
            os_ = diff_attn_latent(hs, cache_diff_k[:, j], cache_diff_v[:, j], diff_w_qkv[j], lam, lam_init,
                                   diff_norm[j], diff_w_o[j])
            new_diff_k.append(k_c)
            new_diff_v.append(v_c)
        else:
            ssd_w = (ssd_w_in[j], ssd_conv_w[j], ssd_conv_b[j], ssd_a_log_f[j], ssd_a_log_b[j],
                     ssd_dt_bias_f[j], ssd_dt_bias_b[j], ssd_d[j], ssd_norm[j], ssd_w_out[j])
            zeros = jnp.zeros((bp, SSD_HEADS, SSD_HEAD_DIM, SSD_STATE), jnp.float32)
            op, s_f, s_b = ssd_mixer(hp, zeros, zeros, *ssd_w)
            os_, _, _ = ssd_mixer(hs, state_ssd_fwd[:, j], state_ssd_bwd[:, j], *ssd_w)
            new_ssd_f.append(s_f)
            new_ssd_b.append(s_b)
        xp = xp + mod_p[2] * op
        xs = xs + mod_s[2] * os_
        ffn_w = (ffn_w_up[i], ffn_conv_w[i], ffn_conv_b[i], ffn_w_down[i])
        hp = modulate(rms_norm(xp, norm_ffn[i]), mod_p[3], mod_p[4])
        hs = modulate(rms_norm(xs, norm_ffn[i]), mod_s[3], mod_s[4])
        xp = xp + mod_p[5] * conv_ffn(hp, *ffn_w)
        xs = xs + mod_s[5] * conv_ffn(hs, *ffn_w)
    y_prompt = rms_norm(xp, final_norm)
    y_sample = rms_norm(xs, final_norm)
    return (y_prompt, y_sample,
            jnp.stack(new_win_k, axis=1), jnp.stack(new_win_v, axis=1),
            jnp.stack(new_gla_f, axis=1), jnp.stack(new_gla_b, axis=1),
            jnp.stack(new_diff_k, axis=1), jnp.stack(new_diff_v, axis=1),
            jnp.stack(new_ssd_f, axis=1), jnp.stack(new_ssd_b, axis=1))
```

```python
import functools
import math

import jax
import jax.numpy as jnp
from jax import lax
from jax.experimental import pallas as pl
from jax.experimental.pallas import tpu as pltpu

F32 = jnp.float32
BF16 = jnp.bfloat16

D_MODEL = 1024
DEPTH = 4
GRID_W = 64
EPS = 1e-6
ROPE_BASE = 10000.0

WA_HEADS = 16
WA_KV_HEADS = 4
WA_GROUP = 4
WA_HEAD_DIM = 64
WINDOW = 128

GLA_HEADS = 4
GLA_DK = 128
GLA_DV = 256
GLA_RANK = 16
GLA_TAU = 16.0
GLA_CHUNK = 64

DA_HEADS = 8
DA_HEAD_DIM = 64

SSD_D_INNER = 2048
SSD_HEAD_DIM = 64
SSD_HEADS = 32
SSD_GROUPS = 4
SSD_STATE = 128
SSD_CHUNK = 128

D_FF = 2816
FF_CHUNK = 256

VMEM_LIMIT_BYTES = 56 * 1024 * 1024
SUBLANES = 8
LANES = 128
NEG_BIG = -1e30


def _cparams(sem):
    return pltpu.CompilerParams(dimension_semantics=sem, vmem_limit_bytes=VMEM_LIMIT_BYTES)


def _sigmoid(x):
    return 1.0 / (1.0 + jnp.exp(-x))


def _silu(x):
    return x * _sigmoid(x)


def _softplus(x):
    return jnp.maximum(x, 0.0) + jnp.log1p(jnp.exp(-jnp.abs(x)))


def _norm_mod(x, gamma, shift, scale):
    ms = jnp.mean(x * x, axis=-1, keepdims=True)
    y = (x * lax.rsqrt(ms + EPS)) * gamma
    return y * (1.0 + scale) + shift


def _split3(x):
    hi = x.astype(BF16)
    r1 = x - hi.astype(F32)
    mid = r1.astype(BF16)
    lo = (r1 - mid.astype(F32)).astype(BF16)
    return hi, mid, lo


def _exact_dot_left01(m01, x):
    hi, mid, lo = _split3(x)
    d = lambda p: jnp.dot(m01, p, preferred_element_type=F32)
    return d(hi) + d(mid) + d(lo)


def _dot_nt(a, b):
    return lax.dot_general(a, b, (((1,), (1,)), ((), ())), preferred_element_type=F32)


def _ada_kernel(c_ref, w_ref, b_ref, o_ref):
    h = _silu(c_ref[...]).astype(BF16)
    o_ref[0] = jnp.dot(h, w_ref[0].astype(BF16), preferred_element_type=F32) + b_ref[0]


def _ada_call(cond, ada_w, ada_b):
    rows = cond.shape[0]
    n = ada_w.shape[-1]
    tn = 1536
    return pl.pallas_call(
        _ada_kernel,
        grid=(DEPTH, n // tn),
        in_specs=[
            pl.BlockSpec((rows, D_MODEL), lambda l, j: (0, 0)),
            pl.BlockSpec((1, D_MODEL, tn), lambda l, j: (l, 0, j)),
            pl.BlockSpec((1, 1, tn), lambda l, j: (l, 0, j)),
        ],
        out_specs=pl.BlockSpec((1, rows, tn), lambda l, j: (l, 0, j)),
        out_shape=jax.ShapeDtypeStruct((DEPTH, rows, n), F32),
        compiler_params=_cparams(("parallel", "parallel")),
        name="ada_mod",
    )(cond, ada_w, ada_b.reshape(DEPTH, 1, n))


def _rope_apply(y, cos, sin):
    lane = lax.broadcasted_iota(jnp.int32, (1, LANES), 1)
    first = (lane % 32) < 16
    partner = jnp.where(first, pltpu.roll(y, LANES - 16, 1), pltpu.roll(y, 16, 1))
    return y * cos + partner * sin


def _fused_proj_kernel(*refs, tm, segs, has_conv, has_rope, n_tiles, chunk):
    it = iter(refs)
    x_ref = next(it)
    if has_conv:
        xp_ref = next(it)
        xn_ref = next(it)
    g_ref = next(it)
    mod_ref = next(it)
    w_ref = next(it)
    if has_rope:
        cos_ref = next(it)
        sin_ref = next(it)
    if has_conv:
        cw_ref = next(it)
    out_refs = [next(it) for _ in segs]
    h_ref = next(it)

    i = pl.program_id(1)
    gamma = g_ref[...]
    shift = mod_ref[0, 0:1, :]
    scale = mod_ref[0, 1:2, :]
    off = SUBLANES if has_conv else 0
    h_ref[off:off + tm, :] = _norm_mod(x_ref[0], gamma, shift, scale).astype(BF16)
    if has_conv:
        hp = _norm_mod(xp_ref[0], gamma, shift, scale)
        hn = _norm_mod(xn_ref[0], gamma, shift, scale)
        h_ref[0:SUBLANES, :] = jnp.where(i > 0, hp, 0.0).astype(BF16)
        h_ref[off + tm:off + tm + SUBLANES, :] = jnp.where(i < n_tiles - 1, hn, 0.0).astype(BF16)

    col = 0
    conv_col = 0
    for seg, o_ref in zip(segs, out_refs):
        width, epi, qscale = seg
        for c0 in range(0, width, chunk):
            wc = min(chunk, width - c0)
            w = w_ref[:, col + c0:col + c0 + wc]
            if epi == "conv_silu":
                u = jnp.dot(h_ref[...], w, preferred_element_type=F32)
                rows = tm + 2 * SUBLANES
                up = pltpu.roll(u, 1, 0)[off:off + tm]
                un = pltpu.roll(u, rows - 1, 0)[off:off + tm]
                uc = u[off:off + tm]
                cw = cw_ref[:, conv_col + c0:conv_col + c0 + wc]
                y = cw[0:1] * up + cw[1:2] * uc + cw[2:3] * un + cw[3:4]
                y = _silu(y)
            else:
                y = jnp.dot(h_ref[off:off + tm, :], w, preferred_element_type=F32)
                if qscale != 1.0:
                    y = y * qscale
                if epi == "rope":
                    cos = cos_ref[...]
                    sin = sin_ref[...]
                    y = jnp.concatenate(
                        [_rope_apply(y[:, k:k + LANES], cos, sin) for k in range(0, wc, LANES)], axis=1)
            o_ref[0, :, c0:c0 + wc] = y.astype(o_ref.dtype)
        col += width
        if epi == "conv_silu":
            conv_col += width


def _fused_proj(x, gamma, mod, w, segs, out_dtypes, *, tm, rope=None, conv_w=None, name):
    b, l, d = x.shape
    n_tiles = l // tm
    has_conv = conv_w is not None
    has_rope = rope is not None
    per_batch = mod.shape[0] > 1
    n_total = w.shape[1]
    bpt = tm // SUBLANES
    nblk8 = l // SUBLANES

    in_specs = [pl.BlockSpec((1, tm, d), lambda bi, i: (bi, i, 0))]
    args = [x]
    if has_conv:
        in_specs.append(pl.BlockSpec((1, SUBLANES, d), lambda bi, i: (bi, jnp.maximum(i * bpt - 1, 0), 0)))
        in_specs.append(pl.BlockSpec((1, SUBLANES, d), lambda bi, i: (bi, jnp.minimum((i + 1) * bpt, nblk8 - 1), 0)))
        args += [x, x]
    in_specs.append(pl.BlockSpec((1, d), lambda bi, i: (0, 0)))
    args.append(gamma.reshape(1, d))
    in_specs.append(pl.BlockSpec((1, 8, d), (lambda bi, i: (bi, 0, 0)) if per_batch else (lambda bi, i: (0, 0, 0))))
    args.append(mod)
    in_specs.append(pl.BlockSpec((d, n_total), lambda bi, i: (0, 0)))
    args.append(w)
    if has_rope:
        in_specs.append(pl.BlockSpec((tm, LANES), lambda bi, i: (i, 0)))
        in_specs.append(pl.BlockSpec((tm, LANES), lambda bi, i: (i, 0)))
        args += [rope[0], rope[1]]
    if has_conv:
        in_specs.append(pl.BlockSpec(conv_w.shape, lambda bi, i: (0, 0)))
        args.append(conv_w)

    out_specs = [pl.BlockSpec((1, tm, s[0]), lambda bi, i: (bi, i, 0)) for s in segs]
    out_shape = [jax.ShapeDtypeStruct((b, l, s[0]), dt) for s, dt in zip(segs, out_dtypes)]
    hrows = tm + (2 * SUBLANES if has_conv else 0)
    kern = functools.partial(_fused_proj_kernel, tm=tm, segs=tuple(segs), has_conv=has_conv,
                             has_rope=has_rope, n_tiles=n_tiles, chunk=512)
    return pl.pallas_call(
        kern,
        grid=(b, n_tiles),
        in_specs=in_specs,
        out_specs=out_specs,
        out_shape=out_shape,
        scratch_shapes=[pltpu.VMEM((hrows, d), BF16)],
        compiler_params=_cparams(("parallel", "parallel")),
        name=name,
    )(*args)


def _out_proj_kernel(o_ref, w_ref, x_ref, mod_ref, out_ref, *, gate_row):
    acc = jnp.dot(o_ref[0].astype(BF16), w_ref[...], preferred_element_type=F32)
    out_ref[0] = x_ref[0] + mod_ref[0, gate_row:gate_row + 1, :] * acc


def _mod_spec(mod):
    if mod.shape[0] > 1:
        return pl.BlockSpec((1, 8, D_MODEL), lambda bi, i: (bi, 0, 0))
    return pl.BlockSpec((1, 8, D_MODEL), lambda bi, i: (0, 0, 0))


def _out_proj(o, w, x, mod, *, tm, name):
    b, l, k = o.shape
    return pl.pallas_call(
        functools.partial(_out_proj_kernel, gate_row=2),
        grid=(b, l // tm),
        in_specs=[
            pl.BlockSpec((1, tm, k), lambda bi, i: (bi, i, 0)),
            pl.BlockSpec((k, D_MODEL), lambda bi, i: (0, 0)),
            pl.BlockSpec((1, tm, D_MODEL), lambda bi, i: (bi, i, 0)),
            _mod_spec(mod),
        ],
        out_specs=pl.BlockSpec((1, tm, D_MODEL), lambda bi, i: (bi, i, 0)),
        out_shape=jax.ShapeDtypeStruct((b, l, D_MODEL), F32),
        compiler_params=_cparams(("parallel", "parallel")),
        name=name,
    )(o, w, x, mod)


def _ffn_kernel(x_ref, xp_ref, xn_ref, g_ref, mod_ref, wg_ref, wv_ref, cwg_ref, cwv_ref, wd_ref, fg_ref,
                out_ref, h_ref, acc_ref, *, tm, n_tiles, n_chunks, final_norm):
    i = pl.program_id(1)
    gamma = g_ref[...]
    shift = mod_ref[0, 3:4, :]
    scale = mod_ref[0, 4:5, :]
    off = SUBLANES
    rows = tm + 2 * SUBLANES
    h_ref[off:off + tm, :] = _norm_mod(x_ref[0], gamma, shift, scale).astype(BF16)
    hp = _norm_mod(xp_ref[0], gamma, shift, scale)
    hn = _norm_mod(xn_ref[0], gamma, shift, scale)
    h_ref[0:SUBLANES, :] = jnp.where(i > 0, hp, 0.0).astype(BF16)
    h_ref[off + tm:rows, :] = jnp.where(i < n_tiles - 1, hn, 0.0).astype(BF16)
    acc_ref[...] = jnp.zeros_like(acc_ref)

    def conv(u, cw):
        up = pltpu.roll(u, 1, 0)[off:off + tm]
        un = pltpu.roll(u, rows - 1, 0)[off:off + tm]
        return cw[0:1] * up + cw[1:2] * u[off:off + tm] + cw[2:3] * un + cw[3:4]

    def body(c, carry):
        h = h_ref[...]
        ug = jnp.dot(h, wg_ref[c], preferred_element_type=F32)
        uv = jnp.dot(h, wv_ref[c], preferred_element_type=F32)
        a = _silu(conv(ug, cwg_ref[c])) * conv(uv, cwv_ref[c])
        acc_ref[...] += jnp.dot(a.astype(BF16), wd_ref[c], preferred_element_type=F32)
        return carry

    lax.fori_loop(0, n_chunks, body, 0)
    y = x_ref[0] + mod_ref[0, 5:6, :] * acc_ref[...]
    if final_norm:
        ms = jnp.mean(y * y, axis=-1, keepdims=True)
        y = (y * lax.rsqrt(ms + EPS)) * fg_ref[...]
    out_ref[0] = y


def _ffn(x, gamma, mod, wg, wv, cwg, cwv, wd, final_gamma, *, tm, final_norm, name):
    b, l, d = x.shape
    n_tiles = l // tm
    n_chunks = wg.shape[0]
    bpt = tm // SUBLANES
    nblk8 = l // SUBLANES
    const3 = lambda bi, i: (0, 0, 0)
    kern = functools.partial(_ffn_kernel, tm=tm, n_tiles=n_tiles, n_chunks=n_chunks, final_norm=final_norm)
    return pl.pallas_call(
        kern,
        grid=(b, n_tiles),
        in_specs=[
            pl.BlockSpec((1, tm, d), lambda bi, i: (bi, i, 0)),
            pl.BlockSpec((1, SUBLANES, d), lambda bi, i: (bi, jnp.maximum(i * bpt - 1, 0), 0)),
            pl.BlockSpec((1, SUBLANES, d), lambda bi, i: (bi, jnp.minimum((i + 1) * bpt, nblk8 - 1), 0)),
            pl.BlockSpec((1, d), lambda bi, i: (0, 0)),
            _mod_spec(mod),
            pl.BlockSpec(wg.shape, const3),
            pl.BlockSpec(wv.shape, const3),
            pl.BlockSpec(cwg.shape, const3),
            pl.BlockSpec(cwv.shape, const3),
            pl.BlockSpec(wd.shape, const3),
            pl.BlockSpec((1, d), lambda bi, i: (0, 0)),
        ],
        out_specs=pl.BlockSpec((1, tm, d), lambda bi, i: (bi, i, 0)),
        out_shape=jax.ShapeDtypeStruct((b, l, d), F32),
        scratch_shapes=[pltpu.VMEM((tm + 2 * SUBLANES, d), BF16), pltpu.VMEM((tm, d), F32)],
        compiler_params=_cparams(("parallel", "parallel")),
        name=name,
    )(x, x, x, gamma.reshape(1, d), mod, wg, wv, cwg, cwv, wd, final_gamma.reshape(1, d))


def _gqa_core(sink_ref, q_ref, kcat, vcat, bias, o_ref, tq):
    lane_head = lax.broadcasted_iota(jnp.int32, (1, WA_KV_HEADS * WA_HEAD_DIM), 1) // WA_HEAD_DIM
    width = WA_KV_HEADS * WA_HEAD_DIM
    for g in range(WA_GROUP):
        qg = q_ref[0, :, g * width:(g + 1) * width]
        q4 = jnp.concatenate([jnp.where(lane_head == h, qg, jnp.zeros_like(qg)) for h in range(WA_KV_HEADS)], axis=0)
        s = _dot_nt(q4, kcat)
        if bias is not None:
            s = s + jnp.concatenate([bias] * WA_KV_HEADS, axis=0)
        sk = jnp.concatenate([jnp.full((tq, 1), sink_ref[h * WA_GROUP + g], F32) for h in range(WA_KV_HEADS)], axis=0)
        m = jnp.maximum(jnp.max(s, axis=-1, keepdims=True), sk)
        p = jnp.exp(s - m)
        denom = jnp.sum(p, axis=-1, keepdims=True) + jnp.exp(sk - m)
        o4 = jnp.dot(p.astype(BF16), vcat, preferred_element_type=F32) / denom
        og = jnp.zeros((tq, width), F32)
        for h in range(WA_KV_HEADS):
            og = og + jnp.where(lane_head == h, o4[h * tq:(h + 1) * tq], 0.0)
        o_ref[0, :, g * width:(g + 1) * width] = og.astype(o_ref.dtype)


def _win_ctx_kernel(sink_ref, q_ref, k_ref, v_ref, o_ref, *, tq):
    _gqa_core(sink_ref, q_ref, k_ref[0].astype(BF16), v_ref[0].astype(BF16), None, o_ref, tq)


def _win_ctx_attn(q, k, v, sink, *, tq):
    b, l, _ = q.shape
    kvw = WA_KV_HEADS * WA_HEAD_DIM
    return pl.pallas_call(
        functools.partial(_win_ctx_kernel, tq=tq),
        grid=(b, l // tq),
        in_specs=[
            pl.BlockSpec(memory_space=pltpu.SMEM),
            pl.BlockSpec((1, tq, D_MODEL), lambda bi, i: (bi, i, 0)),
            pl.BlockSpec((1, l, kvw), lambda bi, i: (bi, 0, 0)),
            pl.BlockSpec((1, l, kvw), lambda bi, i: (bi, 0, 0)),
        ],
        out_specs=pl.BlockSpec((1, tq, D_MODEL), lambda bi, i: (bi, i, 0)),
        out_shape=jax.ShapeDtypeStruct((b, l, D_MODEL), BF16),
        compiler_params=_cparams(("parallel", "parallel")),
        name="win_ctx_attn",
    )(sink, q, k, v)


def _win_lat_kernel(sink_ref, q_ref, kp_ref, kc_ref, kn_ref, vp_ref, vc_ref, vn_ref, kx_ref, vx_ref, o_ref,
                    kcat, vcat, *, tq, seq_len, n_ctx):
    i = pl.program_id(1)
    kcat[0:tq] = kp_ref[0]
    kcat[tq:2 * tq] = kc_ref[0]
    kcat[2 * tq:3 * tq] = kn_ref[0]
    kcat[3 * tq:3 * tq + n_ctx] = kx_ref[0]
    vcat[0:tq] = vp_ref[0]
    vcat[tq:2 * tq] = vc_ref[0]
    vcat[2 * tq:3 * tq] = vn_ref[0]
    vcat[3 * tq:3 * tq + n_ctx] = vx_ref[0]
    nk = 3 * tq + n_ctx
    r = lax.broadcasted_iota(jnp.int32, (tq, nk), 0)
    c = lax.broadcasted_iota(jnp.int32, (tq, nk), 1)
    qpos = i * tq + r
    kpos = (i - 1) * tq + c
    ok = (c >= 3 * tq) | ((kpos >= 0) & (kpos < seq_len) & (jnp.abs(qpos - kpos) <= WINDOW))
    bias = jnp.where(ok, 0.0, NEG_BIG).astype(F32)
    _gqa_core(sink_ref, q_ref, kcat[...], vcat[...], bias, o_ref, tq)


def _win_lat_attn(q, k, v, kctx, vctx, sink):
    b, l, _ = q.shape
    tq = WINDOW
    nq = l // tq
    n_ctx = kctx.shape[1]
    kvw = WA_KV_HEADS * WA_HEAD_DIM
    prev = lambda bi, i: (bi, jnp.maximum(i - 1, 0), 0)
    cur = lambda bi, i: (bi, i, 0)
    nxt = lambda bi, i: (bi, jnp.minimum(i + 1, nq - 1), 0)
    kv_spec = lambda f: pl.BlockSpec((1, tq, kvw), f)
    return pl.pallas_call(
        functools.partial(_win_lat_kernel, tq=tq, seq_len=l, n_ctx=n_ctx),
        grid=(b, nq),
        in_specs=[
            pl.BlockSpec(memory_space=pltpu.SMEM),
            pl.BlockSpec((1, tq, D_MODEL), cur),
            kv_spec(prev), kv_spec(cur), kv_spec(nxt),
            kv_spec(prev), kv_spec(cur), kv_spec(nxt),
            pl.BlockSpec((1, n_ctx, kvw), lambda bi, i: (bi, 0, 0)),
            pl.BlockSpec((1, n_ctx, kvw), lambda bi, i: (bi, 0, 0)),
        ],
        out_specs=pl.BlockSpec((1, tq, D_MODEL), cur),
        out_shape=jax.ShapeDtypeStruct((b, l, D_MODEL), BF16),
        scratch_shapes=[pltpu.VMEM((3 * tq + n_ctx, kvw), BF16), pltpu.VMEM((3 * tq + n_ctx, kvw), BF16)],
        compiler_params=_cparams(("parallel", "parallel")),
        name="win_lat_attn",
    )(sink, q, k, k, k, v, v, v, kctx, vctx)


def _diff_kernel(lqk_ref, gsub_ref, q_ref, k_ref, v_ref, *rest, tq, kc, n_lat, has_ctx, lam_init):
    if has_ctx:
        kx_ref, vx_ref, o_ref = rest
    else:
        (o_ref,) = rest
    hd = DA_HEAD_DIM
    lam = (jnp.exp(jnp.sum(lqk_ref[0:1, :] * lqk_ref[1:2, :], axis=-1, keepdims=True))
           - jnp.exp(jnp.sum(lqk_ref[2:3, :] * lqk_ref[3:4, :], axis=-1, keepdims=True)) + lam_init)
    q = q_ref[0]
    lane = lax.broadcasted_iota(jnp.int32, (1, 2 * hd), 1)
    zero = jnp.zeros_like(q)
    q2 = jnp.concatenate([jnp.where(lane < hd, q, zero), jnp.where(lane >= hd, q, zero)], axis=0)

    def step(kblk, vblk, carry):
        m, l, acc = carry
        s = _dot_nt(q2, kblk)
        m_new = jnp.maximum(m, jnp.max(s, axis=-1, keepdims=True))
        alpha = jnp.exp(m - m_new)
        p = jnp.exp(s - m_new)
        l = alpha * l + jnp.sum(p, axis=-1, keepdims=True)
        acc = alpha * acc + jnp.dot(p.astype(BF16), vblk, preferred_element_type=F32)
        return m_new, l, acc

    def lat_body(j, carry):
        start = pl.multiple_of(j * kc, kc)
        return step(k_ref[0, pl.ds(start, kc), :].astype(BF16), v_ref[0, pl.ds(start, kc), :].astype(BF16), carry)

    carry = (jnp.full((2 * tq, 1), NEG_BIG, F32), jnp.zeros((2 * tq, 1), F32), jnp.zeros((2 * tq, 2 * hd), F32))
    carry = lax.fori_loop(0, n_lat, lat_body, carry)
    if has_ctx:
        carry = step(kx_ref[0], vx_ref[0], carry)
    _, l, acc = carry
    o2 = acc / l
    o = o2[0:tq] - lam * o2[tq:2 * tq]
    ms = jnp.mean(o * o, axis=-1, keepdims=True)
    o = (o * lax.rsqrt(ms + EPS)) * gsub_ref[...] * (1.0 - lam_init)
    o_ref[0] = o.astype(o_ref.dtype)


def _diff_attn(q, k, v, kctx, vctx, lqk, gsub, *, tq, kc, lam_init):
    b, l, _ = q.shape
    hw = 2 * DA_HEAD_DIM
    has_ctx = kctx is not None
    in_specs = [
        pl.BlockSpec((8, hw), lambda bi, h, i: (0, 0)),
        pl.BlockSpec((1, hw), lambda bi, h, i: (0, 0)),
        pl.BlockSpec((1, tq, hw), lambda bi, h, i: (bi, i, h)),
        pl.BlockSpec((1, l, hw), lambda bi, h, i: (bi, 0, h)),
        pl.BlockSpec((1, l, hw), lambda bi, h, i: (bi, 0, h)),
    ]
    args = [lqk, gsub, q, k, v]
    if has_ctx:
        n_ctx = kctx.shape[1]
        in_specs.append(pl.BlockSpec((1, n_ctx, hw), lambda bi, h, i: (bi, 0, h)))
        in_specs.append(pl.BlockSpec((1, n_ctx, hw), lambda bi, h, i: (bi, 0, h)))
        args += [kctx, vctx]
    kern = functools.partial(_diff_kernel, tq=tq, kc=kc, n_lat=l // kc, has_ctx=has_ctx, lam_init=lam_init)
    return pl.pallas_call(
        kern,
        grid=(b, DA_HEADS, l // tq),
        in_specs=in_specs,
        out_specs=pl.BlockSpec((1, tq, hw), lambda bi, h, i: (bi, i, h)),
        out_shape=jax.ShapeDtypeStruct((b, l, D_MODEL), BF16),
        compiler_params=_cparams(("parallel", "parallel", "parallel")),
        name="diff_lat_attn" if has_ctx else "diff_ctx_attn",
    )(*args)


def _gla_scan_kernel(q_ref, k_ref, v_ref, zl_ref, w2_ref, b2_ref, s0_ref, o_ref, sf_ref, s_scr,
                     *, blk, reverse, n_blocks):
    i = pl.program_id(1)
    ch = GLA_CHUNK
    nk = GLA_HEADS * GLA_DK

    @pl.when(i == 0)
    def _():
        s_scr[...] = s0_ref[0]

    r = lax.broadcasted_iota(jnp.int32, (ch, ch), 0)
    c = lax.broadcasted_iota(jnp.int32, (ch, ch), 1)
    keep = (r <= c) if reverse else (r >= c)
    tri = keep.astype(BF16)
    n_sub = blk // ch
    order = range(n_sub - 1, -1, -1) if reverse else range(n_sub)
    edge = 0 if reverse else ch - 1
    for sub in order:
        rows = slice(sub * ch, (sub + 1) * ch)
        z = jnp.dot(zl_ref[0, rows, :].astype(BF16), w2_ref[...], preferred_element_type=F32) + b2_ref[...]
        logg = (jnp.minimum(z, 0.0) - jnp.log1p(jnp.exp(-jnp.abs(z)))) / GLA_TAU
        cum = _exact_dot_left01(tri, logg)
        blast = cum[edge:edge + 1, :]
        e_pos = jnp.exp(cum)
        qd = (q_ref[0, rows, :] * (GLA_DK ** -0.5)) * e_pos
        kd = k_ref[0, rows, :] * jnp.exp(-cum)
        kr = k_ref[0, rows, :] * jnp.exp(blast - cum)
        e_last = jnp.exp(blast)
        for h in range(GLA_HEADS):
            ks = slice(h * GLA_DK, (h + 1) * GLA_DK)
            vs = slice(h * GLA_DV, (h + 1) * GLA_DV)
            qd_h = qd[:, ks].astype(BF16)
            v_h = v_ref[0, rows, vs].astype(BF16)
            att = jnp.where(keep, _dot_nt(qd_h, kd[:, ks].astype(BF16)), 0.0)
            s_h = s_scr[h]
            o_h = (jnp.dot(att.astype(BF16), v_h, preferred_element_type=F32)
                   + jnp.dot(qd_h, s_h.astype(BF16), preferred_element_type=F32))
            o_ref[0, rows, vs] = o_h
            kr_t = jnp.transpose(kr[:, ks]).astype(BF16)
            decay = jnp.transpose(jnp.broadcast_to(e_last[:, ks], (GLA_DK, GLA_DK)))
            decay = jnp.concatenate([decay] * (GLA_DV // GLA_DK), axis=1)
            s_scr[h] = s_h * decay + jnp.dot(kr_t, v_h, preferred_element_type=F32)

    @pl.when(i == n_blocks - 1)
    def _():
        sf_ref[0] = s_scr[...]


def _gla_scan(q, k, v, zl, w2, b2, s0, *, blk, reverse):
    b, l, nk = q.shape
    nv = v.shape[-1]
    n_blocks = l // blk
    blk_idx = (lambda bi, i: (bi, n_blocks - 1 - i, 0)) if reverse else (lambda bi, i: (bi, i, 0))
    state_spec = pl.BlockSpec((1, GLA_HEADS, GLA_DK, GLA_DV), lambda bi, i: (bi, 0, 0, 0))
    return pl.pallas_call(
        functools.partial(_gla_scan_kernel, blk=blk, reverse=reverse, n_blocks=n_blocks),
        grid=(b, n_blocks),
        in_specs=[
            pl.BlockSpec((1, blk, nk), blk_idx),
            pl.BlockSpec((1, blk, nk), blk_idx),
            pl.BlockSpec((1, blk, nv), blk_idx),
            pl.BlockSpec((1, blk, LANES), blk_idx),
            pl.BlockSpec((LANES, nk), lambda bi, i: (0, 0)),
            pl.BlockSpec((1, nk), lambda bi, i: (0, 0)),
            state_spec,
        ],
        out_specs=[pl.BlockSpec((1, blk, nv), blk_idx), state_spec],
        out_shape=[jax.ShapeDtypeStruct((b, l, nv), F32),
                   jax.ShapeDtypeStruct((b, GLA_HEADS, GLA_DK, GLA_DV), F32)],
        scratch_shapes=[pltpu.VMEM((GLA_HEADS, GLA_DK, GLA_DV), F32)],
        compiler_params=_cparams(("parallel", "arbitrary")),
        name="gla_scan_bwd" if reverse else "gla_scan_fwd",
    )(q, k, v, zl, w2, b2, s0)


def _gla_out_kernel(of_ref, ob_ref, r_ref, gh_ref, w_ref, x_ref, mod_ref, out_ref):
    o = of_ref[0] + ob_ref[0]
    parts = []
    for h in range(GLA_HEADS):
        oh = o[:, h * GLA_DV:(h + 1) * GLA_DV]
        ms = jnp.mean(oh * oh, axis=-1, keepdims=True)
        parts.append((oh * lax.rsqrt(ms + EPS)) * gh_ref[...])
    y = jnp.concatenate(parts, axis=1) * _silu(r_ref[0])
    acc = jnp.dot(y.astype(BF16), w_ref[...], preferred_element_type=F32)
    out_ref[0] = x_ref[0] + mod_ref[0, 2:3, :] * acc


def _gla_out(o_f, o_b, r, g_head, w, x, mod, *, tm):
    b, l, nv = o_f.shape
    row = lambda bi, i: (bi, i, 0)
    return pl.pallas_call(
        _gla_out_kernel,
        grid=(b, l // tm),
        in_specs=[
            pl.BlockSpec((1, tm, nv), row),
            pl.BlockSpec((1, tm, nv), row),
            pl.BlockSpec((1, tm, nv), row),
            pl.BlockSpec((1, GLA_DV), lambda bi, i: (0, 0)),
            pl.BlockSpec((nv, D_MODEL), lambda bi, i: (0, 0)),
            pl.BlockSpec((1, tm, D_MODEL), row),
            _mod_spec(mod),
        ],
        out_specs=pl.BlockSpec((1, tm, D_MODEL), row),
        out_shape=jax.ShapeDtypeStruct((b, l, D_MODEL), F32),
        compiler_params=_cparams(("parallel", "parallel")),
        name="gla_out",
    )(o_f, o_b, r, g_head.reshape(1, GLA_DV), w, x, mod)


def _ssd_scan_kernel(x_ref, b_ref, c_ref, dt_ref, bias_ref, a_ref, dsk_ref, s0_ref, y_ref, sf_ref, s_scr,
                     *, reverse, n_chunks, lane_off, add_skip):
    i = pl.program_id(1)
    ch = SSD_CHUNK
    p = SSD_HEAD_DIM
    hpg = SSD_HEADS // SSD_GROUPS

    @pl.when(i == 0)
    def _():
        s_scr[...] = s0_ref[0]

    r = lax.broadcasted_iota(jnp.int32, (ch, ch), 0)
    c = lax.broadcasted_iota(jnp.int32, (ch, ch), 1)
    keep = (r <= c) if reverse else (r >= c)
    tri = keep.astype(BF16)
    edge = 0 if reverse else ch - 1

    dt = _softplus(dt_ref[0] + bias_ref[...])
    la = dt * a_ref[...]
    cum = _exact_dot_left01(tri, la)
    cum_t = jnp.transpose(cum)
    dt_t = jnp.transpose(dt)
    lane = lax.broadcasted_iota(jnp.int32, (1, LANES), 1)
    lo_half = lane < p

    for g in range(SSD_GROUPS):
        bg = b_ref[0, :, g * SSD_STATE:(g + 1) * SSD_STATE].astype(BF16)
        cg = c_ref[0, :, g * SSD_STATE:(g + 1) * SSD_STATE].astype(BF16)
        cb = _dot_nt(cg, bg)
        gs = slice(g * hpg * p, (g + 1) * hpg * p)
        s_g = s_scr[gs, :]
        y_inter = _dot_nt(cg, s_g.astype(BF16))
        xs_parts = []
        for pair in range(hpg // 2):
            cols = slice((g * hpg + 2 * pair) * p, (g * hpg + 2 * pair + 2) * p)
            x_pair = x_ref[0, :, cols]
            ws = []
            e_is = []
            te_s = []
            for t in range(2):
                hl = lane_off + g * hpg + 2 * pair + t
                cum_i = jnp.broadcast_to(cum[:, hl:hl + 1], (ch, ch))
                cum_j = cum_t[hl:hl + 1, :]
                seg = cum_i - cum_j
                decay = jnp.where(keep, jnp.exp(jnp.where(keep, seg, 0.0)), 0.0)
                ws.append((cb * decay * dt_t[hl:hl + 1, :]).astype(BF16))
                e_is.append(jnp.exp(cum_i))
                tot = cum_i[edge:edge + 1, :]
                te_s.append(jnp.exp(tot - cum_i) * jnp.broadcast_to(dt[:, hl:hl + 1], (ch, ch)))
            xb = x_pair.astype(BF16)
            zero = jnp.zeros_like(xb)
            x_bd = jnp.concatenate([jnp.where(lo_half, xb, zero), jnp.where(lo_half, zero, xb)], axis=0)
            y_pair = jnp.dot(jnp.concatenate(ws, axis=1), x_bd, preferred_element_type=F32)
            y_pair = y_pair + y_inter[:, 2 * pair * p:(2 * pair + 2) * p] * jnp.where(lo_half, e_is[0], e_is[1])
            if add_skip:
                hl0 = g * hpg + 2 * pair
                dsk = jnp.where(lo_half, dsk_ref[:, hl0:hl0 + 1], dsk_ref[:, hl0 + 1:hl0 + 2])
                y_pair = y_pair + x_pair * dsk
            y_ref[0, :, cols] = y_pair
            xs_parts.append(x_pair * jnp.where(lo_half, te_s[0], te_s[1]))
        xs = jnp.concatenate(xs_parts, axis=1)
        ds = jnp.dot(jnp.transpose(xs).astype(BF16), bg, preferred_element_type=F32)
        for hh in range(hpg):
            hl = lane_off + g * hpg + hh
            tot = jnp.exp(cum_t[hl:hl + 1, edge:edge + 1])
            rs = slice((g * hpg + hh) * p, (g * hpg + hh + 1) * p)
            s_scr[rs, :] = s_scr[rs, :] * tot + ds[hh * p:(hh + 1) * p, :]

    @pl.when(i == n_chunks - 1)
    def _():
        sf_ref[0] = s_scr[...]


def _ssd_scan(xbc, dt, bias, a, dskip, s0, *, reverse, add_skip):
    b, l, _ = xbc.shape
    ch = SSD_CHUNK
    n_chunks = l // ch
    gn = SSD_GROUPS * SSD_STATE
    rows = SSD_HEADS * SSD_HEAD_DIM

    def at(col):
        if reverse:
            return lambda bi, i: (bi, n_chunks - 1 - i, col)
        return lambda bi, i: (bi, i, col)

    vec = pl.BlockSpec((1, LANES), lambda bi, i: (0, 0))
    state_spec = pl.BlockSpec((1, rows, SSD_STATE), lambda bi, i: (bi, 0, 0))
    kern = functools.partial(_ssd_scan_kernel, reverse=reverse, n_chunks=n_chunks,
                             lane_off=SSD_HEADS if reverse else 0, add_skip=add_skip)
    return pl.pallas_call(
        kern,
        grid=(b, n_chunks),
        in_specs=[
            pl.BlockSpec((1, ch, SSD_D_INNER), at(0)),
            pl.BlockSpec((1, ch, gn), at(SSD_D_INNER // gn)),
            pl.BlockSpec((1, ch, gn), at(SSD_D_INNER // gn + 1)),
            pl.BlockSpec((1, ch, LANES), at(0)),
            vec, vec, vec,
            state_spec,
        ],
        out_specs=[pl.BlockSpec((1, ch, SSD_D_INNER), at(0)), state_spec],
        out_shape=[jax.ShapeDtypeStruct((b, l, SSD_D_INNER), F32),
                   jax.ShapeDtypeStruct((b, rows, SSD_STATE), F32)],
        scratch_shapes=[pltpu.VMEM((rows, SSD_STATE), F32)],
        compiler_params=_cparams(("parallel", "arbitrary")),
        name="ssd_scan_bwd" if reverse else "ssd_scan_fwd",
    )(xbc, xbc, xbc, dt, bias, a, dskip, s0)


def _ssd_out_kernel(yf_ref, yb_ref, z_ref, gn_ref, w_ref, x_ref, mod_ref, out_ref):
    y = (yf_ref[0] + yb_ref[0]) * _silu(z_ref[0])
    ms = jnp.mean(y * y, axis=-1, keepdims=True)
    y = (y * lax.rsqrt(ms + EPS)) * gn_ref[...]
    acc = jnp.dot(y.astype(BF16), w_ref[...], preferred_element_type=F32)
    out_ref[0] = x_ref[0] + mod_ref[0, 2:3, :] * acc


def _ssd_out(y_f, y_b, z, g_norm, w, x, mod, *, tm):
    b, l, di = y_f.shape
    row = lambda bi, i: (bi, i, 0)
    return pl.pallas_call(
        _ssd_out_kernel,
        grid=(b, l // tm),
        in_specs=[
            pl.BlockSpec((1, tm, di), row),
            pl.BlockSpec((1, tm, di), row),
            pl.BlockSpec((1, tm, di), row),
            pl.BlockSpec((1, di), lambda bi, i: (0, 0)),
            pl.BlockSpec((di, D_MODEL), lambda bi, i: (0, 0)),
            pl.BlockSpec((1, tm, D_MODEL), row),
            _mod_spec(mod),
        ],
        out_specs=pl.BlockSpec((1, tm, D_MODEL), row),
        out_shape=jax.ShapeDtypeStruct((b, l, D_MODEL), F32),
        compiler_params=_cparams(("parallel", "parallel")),
        name="ssd_out",
    )(y_f, y_b, z, g_norm.reshape(1, di), w, x, mod)


def _rope_tables(n_tokens, dim):
    rows = n_tokens // GRID_W
    row = jnp.repeat(jnp.arange(rows, dtype=F32), GRID_W)
    col = jnp.tile(jnp.arange(GRID_W, dtype=F32), rows)
    axis_dim = dim // 2
    inv = ROPE_BASE ** (-jnp.arange(0, axis_dim, 2, dtype=F32) / axis_dim)
    ar = row[:, None] * inv
    ac = col[:, None] * inv
    cos = jnp.concatenate([jnp.cos(ar), jnp.cos(ar), jnp.cos(ac), jnp.cos(ac)], axis=1)
    sin = jnp.concatenate([-jnp.sin(ar), jnp.sin(ar), -jnp.sin(ac), jnp.sin(ac)], axis=1)
    reps = LANES // dim
    return jnp.tile(cos, (1, reps)), jnp.tile(sin, (1, reps))


def _pad_cols(w, width):
    return jnp.pad(w, ((0, 0), (0, width - w.shape[1])))


def _conv_pack(conv_w, conv_b):
    return jnp.concatenate([conv_w, conv_b[None], jnp.zeros((4, conv_w.shape[1]), F32)], axis=0)


def _tm_for(l):
    return min(l, 512)


def kernel(x_prompt, x_sample, c, c_ctx, cache_win_k, cache_win_v, state_gla_fwd, state_gla_bwd, cache_diff_k, cache_diff_v, state_ssd_fwd, state_ssd_bwd, ada_w, ada_b, norm_mix, norm_ffn, ffn_w_up, ffn_conv_w, ffn_conv_b, ffn_w_down, final_norm, win_w_qkv, win_w_o, win_sink, gla_w_qkvr, gla_w_gf1, gla_w_gf2, gla_b_gf, gla_w_gb1, gla_w_gb2, gla_b_gb, gla_norm, gla_w_o, diff_w_qkv, diff_lq1, diff_lk1, diff_lq2, diff_lk2, diff_norm, diff_w_o, ssd_w_in, ssd_conv_w, ssd_conv_b, ssd_a_log_f, ssd_a_log_b, ssd_dt_bias_f, ssd_dt_bias_b, ssd_d, ssd_norm, ssd_w_out):
    xp, xs = x_prompt, x_sample
    bp, lp, d = xp.shape
    bs, ls, _ = xs.shape
    tmp, tms = _tm_for(lp), _tm_for(ls)

    n_cond = 1 + bs
    cond_rows = -(-n_cond // SUBLANES) * SUBLANES
    cond = jnp.concatenate([c_ctx[None], c, jnp.zeros((cond_rows - n_cond, d), F32)], axis=0)
    mods = _ada_call(cond, ada_w, ada_b).reshape(DEPTH, cond_rows, 6, d)
    mods = jnp.pad(mods, ((0, 0), (0, 0), (0, 2), (0, 0)))

    outs = {}
    for i in range(DEPTH):
        kind, j = i % 4, i // 4
        mod_p = mods[i, 0:1]
        mod_s = mods[i, 1:1 + bs]
        if kind == 0:
            nq = WA_HEADS * WA_HEAD_DIM
            nkv = WA_KV_HEADS * WA_HEAD_DIM
            perm = jnp.arange(nq).reshape(WA_KV_HEADS, WA_GROUP, WA_HEAD_DIM).transpose(1, 0, 2).reshape(-1)
            wq = win_w_qkv[j][:, :nq][:, perm]
            w = jnp.concatenate([wq, win_w_qkv[j][:, nq:]], axis=1).astype(BF16)
            w_o = win_w_o[j][perm, :].astype(BF16)
            sink = win_sink[j]
            qscale = WA_HEAD_DIM ** -0.5
            q, k, v = _fused_proj(xp, norm_mix[i], mod_p, w,
                                  [(nq, "plain", qscale), (nkv, "plain", 1.0), (nkv, "plain", 1.0)],
                                  [BF16, F32, F32], tm=tmp, name="win_proj_ctx")
            outs["win_k"] = k.reshape(bp, 1, lp, WA_KV_HEADS, WA_HEAD_DIM)
            outs["win_v"] = v.reshape(bp, 1, lp, WA_KV_HEADS, WA_HEAD_DIM)
            o = _win_ctx_attn(q, k, v, sink, tq=min(lp, 128))
            xp = _out_proj(o, w_o, xp, mod_p, tm=tmp, name="win_out_ctx")
            rope = _rope_tables(ls, WA_HEAD_DIM)
            q, k, v = _fused_proj(xs, norm_mix[i], mod_s, w,
                                  [(nq, "rope", qscale), (nkv, "rope", 1.0), (nkv, "plain", 1.0)],
                                  [BF16, BF16, BF16], tm=tms, rope=rope, name="win_proj_lat")
            n_ctx = cache_win_k.shape[2]
            kctx = cache_win_k[:, j].reshape(bs, n_ctx, nkv).astype(BF16)
            vctx = cache_win_v[:, j].reshape(bs, n_ctx, nkv).astype(BF16)
            o = _win_lat_attn(q, k, v, kctx, vctx, sink)
            xs = _out_proj(o, w_o, xs, mod_s, tm=tms, name="win_out_lat")
        elif kind == 1:
            nk = GLA_HEADS * GLA_DK
            nv = GLA_HEADS * GLA_DV
            w1 = _pad_cols(jnp.concatenate([gla_w_gf1[j], gla_w_gb1[j]], axis=1), LANES)
            w = jnp.concatenate([gla_w_qkvr[j], w1], axis=1).astype(BF16)
            segs = [(nk, "plain", 1.0), (nk, "plain", 1.0), (nv, "plain", 1.0), (nv, "plain", 1.0),
                    (LANES, "plain", 1.0)]
            zrows = jnp.zeros((LANES - 2 * GLA_RANK, nk), F32)
            w2_f = jnp.concatenate([gla_w_gf2[j], jnp.zeros((GLA_RANK, nk), F32), zrows], axis=0).astype(BF16)
            w2_b = jnp.concatenate([jnp.zeros((GLA_RANK, nk), F32), gla_w_gb2[j], zrows], axis=0).astype(BF16)
            b2_f = gla_b_gf[j].reshape(1, nk)
            b2_b = gla_b_gb[j].reshape(1, nk)
            w_o = gla_w_o[j].astype(BF16)
            for stream in ("p", "s"):
                if stream == "p":
                    x, mod, tm, bsz = xp, mod_p, tmp, bp
                    s0_f = jnp.zeros((bp, GLA_HEADS, GLA_DK, GLA_DV), F32)
                    s0_b = s0_f
                else:
                    x, mod, tm, bsz = xs, mod_s, tms, bs
                    s0_f = state_gla_fwd[:, j]
                    s0_b = state_gla_bwd[:, j]
                q, k, v, r, zl = _fused_proj(x, norm_mix[i], mod, w, segs, [F32] * 5, tm=tm,
                                             name="gla_proj_" + stream)
                blk = min(x.shape[1], 256)
                o_f, s_f = _gla_scan(q, k, v, zl, w2_f, b2_f, s0_f, blk=blk, reverse=False)
                o_b, s_b = _gla_scan(q, k, v, zl, w2_b, b2_b, s0_b, blk=blk, reverse=True)
                x = _gla_out(o_f, o_b, r, gla_norm[j], w_o, x, mod, tm=tm)
                if stream == "p":
                    xp = x
                    outs["gla_f"] = s_f[:, None]
                    outs["gla_b"] = s_b[:, None]
                else:
                    xs = x
        elif kind == 2:
            lam_init = 0.8 - 0.6 * math.exp(-0.3 * i)
            nh = DA_HEADS * 2 * DA_HEAD_DIM
            w = diff_w_qkv[j].astype(BF16)
            w_o = diff_w_o[j].astype(BF16)
            qscale = DA_HEAD_DIM ** -0.5
            lqk = jnp.stack([diff_lq1[j], diff_lk1[j], diff_lq2[j], diff_lk2[j]], axis=0)
            lqk = jnp.pad(lqk, ((0, 4), (0, 2 * DA_HEAD_DIM - lqk.shape[1])))
            gsub = diff_norm[j].reshape(1, 2 * DA_HEAD_DIM)
            q, k, v = _fused_proj(xp, norm_mix[i], mod_p, w,
                                  [(nh, "plain", qscale), (nh, "plain", 1.0), (nh, "plain", 1.0)],
                                  [BF16, F32, F32], tm=tmp, name="diff_proj_ctx")
            outs["diff_k"] = k.reshape(bp, 1, lp, DA_HEADS, 2, DA_HEAD_DIM)
            outs["diff_v"] = v.reshape(bp, 1, lp, DA_HEADS, 2 * DA_HEAD_DIM)
            o = _diff_attn(q, k, v, None, None, lqk, gsub, tq=min(lp, 256), kc=min(lp, 512), lam_init=lam_init)
            xp = _out_proj(o, w_o, xp, mod_p, tm=tmp, name="diff_out_ctx")
            rope = _rope_tables(ls, DA_HEAD_DIM)
            q, k, v = _fused_proj(xs, norm_mix[i], mod_s, w,
                                  [(nh, "rope", qscale), (nh, "rope", 1.0), (nh, "plain", 1.0)],
                                  [BF16, BF16, BF16], tm=tms, rope=rope, name="diff_proj_lat")
            n_ctx = cache_diff_k.shape[2]
            kctx = cache_diff_k[:, j].reshape(bs, n_ctx, nh).astype(BF16)
            vctx = cache_diff_v[:, j].reshape(bs, n_ctx, nh).astype(BF16)
            o = _diff_attn(q, k, v, kctx, vctx, lqk, gsub, tq=min(ls, 256), kc=min(ls, 512), lam_init=lam_init)
            xs = _out_proj(o, w_o, xs, mod_s, tm=tms, name="diff_out_lat")
        else:
            gn = SSD_GROUPS * SSD_STATE
            nxbc = SSD_D_INNER + 2 * gn
            w_in = ssd_w_in[j]
            w = jnp.concatenate([w_in[:, :SSD_D_INNER + nxbc], _pad_cols(w_in[:, SSD_D_INNER + nxbc:], LANES)],
                                axis=1).astype(BF16)
            segs = [(SSD_D_INNER, "plain", 1.0), (nxbc, "conv_silu", 1.0), (LANES, "plain", 1.0)]
            cw = _conv_pack(ssd_conv_w[j], ssd_conv_b[j])
            zpad = jnp.zeros((LANES - 2 * SSD_HEADS,), F32)
            zh = jnp.zeros((SSD_HEADS,), F32)
            bias = jnp.concatenate([ssd_dt_bias_f[j], ssd_dt_bias_b[j], zpad]).reshape(1, LANES)
            a_f = jnp.concatenate([-jnp.exp(ssd_a_log_f[j]), zh, zpad]).reshape(1, LANES)
            a_b = jnp.concatenate([zh, -jnp.exp(ssd_a_log_b[j]), zpad]).reshape(1, LANES)
            dsk = jnp.concatenate([ssd_d[j], zh, zpad]).reshape(1, LANES)
            w_out = ssd_w_out[j].astype(BF16)
            rows = SSD_HEADS * SSD_HEAD_DIM
            for stream in ("p", "s"):
                if stream == "p":
                    x, mod, tm, bsz = xp, mod_p, tmp, bp
                    s0_f = jnp.zeros((bp, rows, SSD_STATE), F32)
                    s0_b = s0_f
                else:
                    x, mod, tm, bsz = xs, mod_s, tms, bs
                    s0_f = state_ssd_fwd[:, j].reshape(bs, rows, SSD_STATE)
                    s0_b = state_ssd_bwd[:, j].reshape(bs, rows, SSD_STATE)
                z, xbc, dt = _fused_proj(x, norm_mix[i], mod, w, segs, [F32] * 3, tm=min(tm, 256), conv_w=cw,
                                         name="ssd_proj_" + stream)
                y_f, s_f = _ssd_scan(xbc, dt, bias, a_f, dsk, s0_f, reverse=False, add_skip=True)
                y_b, s_b = _ssd_scan(xbc, dt, bias, a_b, dsk, s0_b, reverse=True, add_skip=False)
                x = _ssd_out(y_f, y_b, z, ssd_norm[j], w_out, x, mod, tm=tm)
                if stream == "p":
                    xp = x
                    outs["ssd_f"] = s_f.reshape(bp, 1, SSD_HEADS, SSD_HEAD_DIM, SSD_STATE)
                    outs["ssd_b"] = s_b.reshape(bp, 1, SSD_HEADS, SSD_HEAD_DIM, SSD_STATE)
                else:
                    xs = x

        n_ch = D_FF // FF_CHUNK
        w_up = ffn_w_up[i].astype(BF16)
        wg = w_up[:, :D_FF].reshape(d, n_ch, FF_CHUNK).transpose(1, 0, 2)
        wv = w_up[:, D_FF:].reshape(d, n_ch, FF_CHUNK).transpose(1, 0, 2)
        cw = _conv_pack(ffn_conv_w[i], ffn_conv_b[i])
        cwg = cw[:, :D_FF].reshape(8, n_ch, FF_CHUNK).transpose(1, 0, 2)
        cwv = cw[:, D_FF:].reshape(8, n_ch, FF_CHUNK).transpose(1, 0, 2)
        wd = ffn_w_down[i].astype(BF16).reshape(n_ch, FF_CHUNK, d)
        last = i == DEPTH - 1
        xp = _ffn(xp, norm_ffn[i], mod_p, wg, wv, cwg, cwv, wd, final_norm, tm=tmp, final_norm=last, name="ffn_p")
        xs = _ffn(xs, norm_ffn[i], mod_s, wg, wv, cwg, cwv, wd, final_norm, tm=tms, final_norm=last, name="ffn_s")

    return (xp, xs, outs["win_k"], outs["win_v"], outs["gla_f"], outs["gla_b"],
            outs["diff_k"], outs["diff_v"], outs["ssd_f"], outs["ssd_b"])
```

```python
import functools
import math

import jax
import jax.numpy as jnp
from jax import lax
from jax.experimental import pallas as pl
from jax.experimental.pallas import tpu as pltpu

F32 = jnp.float32
BF16 = jnp.bfloat16

D_MODEL = 1024
DEPTH = 4
GRID_W = 64
EPS = 1e-6
ROPE_BASE = 10000.0

WA_HEADS = 16
WA_KV_HEADS = 4
WA_GROUP = 4
WA_HEAD_DIM = 64
WINDOW = 128

GLA_HEADS = 4
GLA_DK = 128
GLA_DV = 256
GLA_RANK = 16
GLA_TAU = 16.0
GLA_CHUNK = 64

DA_HEADS = 8
DA_HEAD_DIM = 64

SSD_D_INNER = 2048
SSD_HEAD_DIM = 64
SSD_HEADS = 32
SSD_GROUPS = 4
SSD_STATE = 128
SSD_CHUNK = 128

D_FF = 2816
FF_CHUNK = 256
FF_GROUP = 4

VMEM_LIMIT_BYTES = 56 * 1024 * 1024
SUBLANES = 8
LANES = 128
NEG_BIG = -1e30


def _cparams(sem):
    return pltpu.CompilerParams(dimension_semantics=sem, vmem_limit_bytes=VMEM_LIMIT_BYTES)


def _sigmoid(x):
    return 1.0 / (1.0 + jnp.exp(-x))


def _silu(x):
    return x * _sigmoid(x)


def _softplus(x):
    return jnp.maximum(x, 0.0) + jnp.log1p(jnp.exp(-jnp.abs(x)))


def _norm_mod(x, gamma, shift, scale):
    ms = jnp.mean(x * x, axis=-1, keepdims=True)
    y = (x * lax.rsqrt(ms + EPS)) * gamma
    return y * (1.0 + scale) + shift


def _split3(x):
    hi = x.astype(BF16)
    r1 = x - hi.astype(F32)
    mid = r1.astype(BF16)
    lo = (r1 - mid.astype(F32)).astype(BF16)
    return hi, mid, lo


def _exact_dot_left01(m01, x):
    hi, mid, lo = _split3(x)
    d = lambda p: jnp.dot(m01, p, preferred_element_type=F32)
    return d(hi) + d(mid) + d(lo)


def _dot_nt(a, b):
    return lax.dot_general(a, b, (((1,), (1,)), ((), ())), preferred_element_type=F32)


def _ada_kernel(c_ref, w_ref, b_ref, o_ref):
    h = _silu(c_ref[...]).astype(BF16)
    o_ref[0] = jnp.dot(h, w_ref[0].astype(BF16), preferred_element_type=F32) + b_ref[0]


def _ada_call(cond, ada_w, ada_b):
    rows = cond.shape[0]
    n = ada_w.shape[-1]
    tn = 1536
    return pl.pallas_call(
        _ada_kernel,
        grid=(DEPTH, n // tn),
        in_specs=[
            pl.BlockSpec((rows, D_MODEL), lambda l, j: (0, 0)),
            pl.BlockSpec((1, D_MODEL, tn), lambda l, j: (l, 0, j)),
            pl.BlockSpec((1, 1, tn), lambda l, j: (l, 0, j)),
        ],
        out_specs=pl.BlockSpec((1, rows, tn), lambda l, j: (l, 0, j)),
        out_shape=jax.ShapeDtypeStruct((DEPTH, rows, n), F32),
        compiler_params=_cparams(("parallel", "parallel")),
        name="ada_mod",
    )(cond, ada_w, ada_b.reshape(DEPTH, 1, n))


def _rope_apply(y, cos, sin):
    lane = lax.broadcasted_iota(jnp.int32, (1, LANES), 1)
    first = (lane % 32) < 16
    partner = jnp.where(first, pltpu.roll(y, LANES - 16, 1), pltpu.roll(y, 16, 1))
    return y * cos + partner * sin


def _fused_proj_kernel(*refs, tm, segs, has_conv, has_rope, n_tiles, chunk):
    it = iter(refs)
    x_ref = next(it)
    if has_conv:
        xp_ref = next(it)
        xn_ref = next(it)
    g_ref = next(it)
    mod_ref = next(it)
    w_ref = next(it)
    if has_rope:
        cos_ref = next(it)
        sin_ref = next(it)
    if has_conv:
        cw_ref = next(it)
    out_refs = [next(it) for _ in segs]
    h_ref = next(it)

    i = pl.program_id(1)
    gamma = g_ref[...]
    shift = mod_ref[0, 0:1, :]
    scale = mod_ref[0, 1:2, :]
    off = SUBLANES if has_conv else 0
    h_ref[off:off + tm, :] = _norm_mod(x_ref[0], gamma, shift, scale).astype(BF16)
    if has_conv:
        hp = _norm_mod(xp_ref[0], gamma, shift, scale)
        hn = _norm_mod(xn_ref[0], gamma, shift, scale)
        h_ref[0:SUBLANES, :] = jnp.where(i > 0, hp, 0.0).astype(BF16)
        h_ref[off + tm:off + tm + SUBLANES, :] = jnp.where(i < n_tiles - 1, hn, 0.0).astype(BF16)

    col = 0
    conv_col = 0
    for seg, o_ref in zip(segs, out_refs):
        width, epi, qscale = seg
        for c0 in range(0, width, chunk):
            wc = min(chunk, width - c0)
            w = w_ref[:, col + c0:col + c0 + wc]
            if epi == "conv_silu":
                u = jnp.dot(h_ref[...], w, preferred_element_type=F32)
                rows = tm + 2 * SUBLANES
                up = pltpu.roll(u, 1, 0)[off:off + tm]
                un = pltpu.roll(u, rows - 1, 0)[off:off + tm]
                uc = u[off:off + tm]
                cw = cw_ref[:, conv_col + c0:conv_col + c0 + wc]
                y = cw[0:1] * up + cw[1:2] * uc + cw[2:3] * un + cw[3:4]
                y = _silu(y)
            else:
                y = jnp.dot(h_ref[off:off + tm, :], w, preferred_element_type=F32)
                if qscale != 1.0:
                    y = y * qscale
                if epi == "rope":
                    cos = cos_ref[...]
                    sin = sin_ref[...]
                    y = jnp.concatenate(
                        [_rope_apply(y[:, k:k + LANES], cos, sin) for k in range(0, wc, LANES)], axis=1)
            o_ref[0, :, c0:c0 + wc] = y.astype(o_ref.dtype)
        col += width
        if epi == "conv_silu":
            conv_col += width


def _fused_proj(x, gamma, mod, w, segs, out_dtypes, *, tm, rope=None, conv_w=None, name):
    b, l, d = x.shape
    n_tiles = l // tm
    has_conv = conv_w is not None
    has_rope = rope is not None
    per_batch = mod.shape[0] > 1
    n_total = w.shape[1]
    bpt = tm // SUBLANES
    nblk8 = l // SUBLANES

    in_specs = [pl.BlockSpec((1, tm, d), lambda bi, i: (bi, i, 0))]
    args = [x]
    if has_conv:
        in_specs.append(pl.BlockSpec((1, SUBLANES, d), lambda bi, i: (bi, jnp.maximum(i * bpt - 1, 0), 0)))
        in_specs.append(pl.BlockSpec((1, SUBLANES, d), lambda bi, i: (bi, jnp.minimum((i + 1) * bpt, nblk8 - 1), 0)))
        args += [x, x]
    in_specs.append(pl.BlockSpec((1, d), lambda bi, i: (0, 0)))
    args.append(gamma.reshape(1, d))
    in_specs.append(pl.BlockSpec((1, 8, d), (lambda bi, i: (bi, 0, 0)) if per_batch else (lambda bi, i: (0, 0, 0))))
    args.append(mod)
    in_specs.append(pl.BlockSpec((d, n_total), lambda bi, i: (0, 0)))
    args.append(w)
    if has_rope:
        in_specs.append(pl.BlockSpec((tm, LANES), lambda bi, i: (i, 0)))
        in_specs.append(pl.BlockSpec((tm, LANES), lambda bi, i: (i, 0)))
        args += [rope[0], rope[1]]
    if has_conv:
        in_specs.append(pl.BlockSpec(conv_w.shape, lambda bi, i: (0, 0)))
        args.append(conv_w)

    out_specs = [pl.BlockSpec((1, tm, s[0]), lambda bi, i: (bi, i, 0)) for s in segs]
    out_shape = [jax.ShapeDtypeStruct((b, l, s[0]), dt) for s, dt in zip(segs, out_dtypes)]
    hrows = tm + (2 * SUBLANES if has_conv else 0)
    kern = functools.partial(_fused_proj_kernel, tm=tm, segs=tuple(segs), has_conv=has_conv,
                             has_rope=has_rope, n_tiles=n_tiles, chunk=512)
    return pl.pallas_call(
        kern,
        grid=(b, n_tiles),
        in_specs=in_specs,
        out_specs=out_specs,
        out_shape=out_shape,
        scratch_shapes=[pltpu.VMEM((hrows, d), BF16)],
        compiler_params=_cparams(("parallel", "parallel")),
        name=name,
    )(*args)


def _out_proj_kernel(o_ref, w_ref, x_ref, mod_ref, out_ref, *, gate_row):
    acc = jnp.dot(o_ref[0].astype(BF16), w_ref[...], preferred_element_type=F32)
    out_ref[0] = x_ref[0] + mod_ref[0, gate_row:gate_row + 1, :] * acc


def _mod_spec(mod):
    if mod.shape[0] > 1:
        return pl.BlockSpec((1, 8, D_MODEL), lambda bi, i: (bi, 0, 0))
    return pl.BlockSpec((1, 8, D_MODEL), lambda bi, i: (0, 0, 0))


def _out_proj(o, w, x, mod, *, tm, name):
    b, l, k = o.shape
    return pl.pallas_call(
        functools.partial(_out_proj_kernel, gate_row=2),
        grid=(b, l // tm),
        in_specs=[
            pl.BlockSpec((1, tm, k), lambda bi, i: (bi, i, 0)),
            pl.BlockSpec((k, D_MODEL), lambda bi, i: (0, 0)),
            pl.BlockSpec((1, tm, D_MODEL), lambda bi, i: (bi, i, 0)),
            _mod_spec(mod),
        ],
        out_specs=pl.BlockSpec((1, tm, D_MODEL), lambda bi, i: (bi, i, 0)),
        out_shape=jax.ShapeDtypeStruct((b, l, D_MODEL), F32),
        compiler_params=_cparams(("parallel", "parallel")),
        name=name,
    )(o, w, x, mod)


def _ffn_kernel(x_ref, xp_ref, xn_ref, g_ref, mod_ref, wu_ref, cw_ref, wd_ref, fg_ref,
                out_ref, h_ref, acc_ref, act_ref, ug_a, uv_a, ug_b, uv_b, *, tm, n_tiles, final_norm):
    i = pl.program_id(1)
    gamma = g_ref[...]
    shift = mod_ref[0, 3:4, :]
    scale = mod_ref[0, 4:5, :]
    off = SUBLANES
    rows = tm + 2 * SUBLANES
    h_ref[off:off + tm, :] = _norm_mod(x_ref[0], gamma, shift, scale).astype(BF16)
    hp = _norm_mod(xp_ref[0], gamma, shift, scale)
    hn = _norm_mod(xn_ref[0], gamma, shift, scale)
    h_ref[0:SUBLANES, :] = jnp.where(i > 0, hp, 0.0).astype(BF16)
    h_ref[off + tm:rows, :] = jnp.where(i < n_tiles - 1, hn, 0.0).astype(BF16)

    def conv(u_ref, cw):
        return (cw[0:1] * u_ref[off - 1:off - 1 + tm, :] + cw[1:2] * u_ref[off:off + tm, :]
                + cw[2:3] * u_ref[off + 1:off + 1 + tm, :] + cw[3:4])

    n_chunks = D_FF // FF_CHUNK
    group_start = 0
    for c in range(n_chunks):
        ug_ref, uv_ref = (ug_a, uv_a) if c % 2 == 0 else (ug_b, uv_b)
        gs = slice(c * FF_CHUNK, (c + 1) * FF_CHUNK)
        vs = slice(D_FF + c * FF_CHUNK, D_FF + (c + 1) * FF_CHUNK)
        h = h_ref[...]
        ug_ref[...] = jnp.dot(h, wu_ref[:, gs], preferred_element_type=F32)
        uv_ref[...] = jnp.dot(h, wu_ref[:, vs], preferred_element_type=F32)
        a = _silu(conv(ug_ref, cw_ref[:, gs])) * conv(uv_ref, cw_ref[:, vs])
        act_ref[:, gs] = a.astype(BF16)
        if (c + 1) % FF_GROUP == 0 or c == n_chunks - 1:
            ks = slice(group_start * FF_CHUNK, (c + 1) * FF_CHUNK)
            part = jnp.dot(act_ref[:, ks], wd_ref[ks, :], preferred_element_type=F32)
            if group_start == 0:
                acc_ref[...] = part
            else:
                acc_ref[...] += part
            group_start = c + 1
    y = x_ref[0] + mod_ref[0, 5:6, :] * acc_ref[...]
    if final_norm:
        ms = jnp.mean(y * y, axis=-1, keepdims=True)
        y = (y * lax.rsqrt(ms + EPS)) * fg_ref[...]
    out_ref[0] = y


def _ffn(x, gamma, mod, w_up, cw, w_down, final_gamma, *, tm, final_norm, name):
    b, l, d = x.shape
    n_tiles = l // tm
    bpt = tm // SUBLANES
    nblk8 = l // SUBLANES
    const2 = lambda bi, i: (0, 0)
    rows = tm + 2 * SUBLANES
    kern = functools.partial(_ffn_kernel, tm=tm, n_tiles=n_tiles, final_norm=final_norm)
    return pl.pallas_call(
        kern,
        grid=(b, n_tiles),
        in_specs=[
            pl.BlockSpec((1, tm, d), lambda bi, i: (bi, i, 0)),
            pl.BlockSpec((1, SUBLANES, d), lambda bi, i: (bi, jnp.maximum(i * bpt - 1, 0), 0)),
            pl.BlockSpec((1, SUBLANES, d), lambda bi, i: (bi, jnp.minimum((i + 1) * bpt, nblk8 - 1), 0)),
            pl.BlockSpec((1, d), const2),
            _mod_spec(mod),
            pl.BlockSpec(w_up.shape, const2),
            pl.BlockSpec(cw.shape, const2),
            pl.BlockSpec(w_down.shape, const2),
            pl.BlockSpec((1, d), const2),
        ],
        out_specs=pl.BlockSpec((1, tm, d), lambda bi, i: (bi, i, 0)),
        out_shape=jax.ShapeDtypeStruct((b, l, d), F32),
        scratch_shapes=[pltpu.VMEM((rows, d), BF16), pltpu.VMEM((tm, d), F32), pltpu.VMEM((tm, D_FF), BF16)]
        + [pltpu.VMEM((rows, FF_CHUNK), F32)] * 4,
        compiler_params=_cparams(("parallel", "parallel")),
        name=name,
    )(x, x, x, gamma.reshape(1, d), mod, w_up, cw, w_down, final_gamma.reshape(1, d))


def _gqa_core(sink_ref, q_ref, kcat, vcat, bias, o_ref, tq):
    lane_head = lax.broadcasted_iota(jnp.int32, (1, WA_KV_HEADS * WA_HEAD_DIM), 1) // WA_HEAD_DIM
    width = WA_KV_HEADS * WA_HEAD_DIM
    for g in range(WA_GROUP):
        qg = q_ref[0, :, g * width:(g + 1) * width]
        q4 = jnp.concatenate([jnp.where(lane_head == h, qg, jnp.zeros_like(qg)) for h in range(WA_KV_HEADS)], axis=0)
        s = _dot_nt(q4, kcat)
        if bias is not None:
            s = s + jnp.concatenate([bias] * WA_KV_HEADS, axis=0)
        sk = jnp.concatenate([jnp.full((tq, 1), sink_ref[h * WA_GROUP + g], F32) for h in range(WA_KV_HEADS)], axis=0)
        m = jnp.maximum(jnp.max(s, axis=-1, keepdims=True), sk)
        p = jnp.exp(s - m)
        denom = jnp.sum(p, axis=-1, keepdims=True) + jnp.exp(sk - m)
        o4 = jnp.dot(p.astype(BF16), vcat, preferred_element_type=F32) / denom
        og = jnp.zeros((tq, width), F32)
        for h in range(WA_KV_HEADS):
            og = og + jnp.where(lane_head == h, o4[h * tq:(h + 1) * tq], 0.0)
        o_ref[0, :, g * width:(g + 1) * width] = og.astype(o_ref.dtype)


def _win_ctx_kernel(sink_ref, q_ref, k_ref, v_ref, o_ref, *, tq):
    _gqa_core(sink_ref, q_ref, k_ref[0].astype(BF16), v_ref[0].astype(BF16), None, o_ref, tq)


def _win_ctx_attn(q, k, v, sink, *, tq):
    b, l, _ = q.shape
    kvw = WA_KV_HEADS * WA_HEAD_DIM
    return pl.pallas_call(
        functools.partial(_win_ctx_kernel, tq=tq),
        grid=(b, l // tq),
        in_specs=[
            pl.BlockSpec(memory_space=pltpu.SMEM),
            pl.BlockSpec((1, tq, D_MODEL), lambda bi, i: (bi, i, 0)),
            pl.BlockSpec((1, l, kvw), lambda bi, i: (bi, 0, 0)),
            pl.BlockSpec((1, l, kvw), lambda bi, i: (bi, 0, 0)),
        ],
        out_specs=pl.BlockSpec((1, tq, D_MODEL), lambda bi, i: (bi, i, 0)),
        out_shape=jax.ShapeDtypeStruct((b, l, D_MODEL), BF16),
        compiler_params=_cparams(("parallel", "parallel")),
        name="win_ctx_attn",
    )(sink, q, k, v)


def _win_lat_kernel(sink_ref, q_ref, kp_ref, kc_ref, kn_ref, vp_ref, vc_ref, vn_ref, kx_ref, vx_ref, o_ref,
                    kcat, vcat, *, tq, seq_len, n_ctx):
    i = pl.program_id(1)
    kcat[0:tq] = kp_ref[0]
    kcat[tq:2 * tq] = kc_ref[0]
    kcat[2 * tq:3 * tq] = kn_ref[0]
    kcat[3 * tq:3 * tq + n_ctx] = kx_ref[0]
    vcat[0:tq] = vp_ref[0]
    vcat[tq:2 * tq] = vc_ref[0]
    vcat[2 * tq:3 * tq] = vn_ref[0]
    vcat[3 * tq:3 * tq + n_ctx] = vx_ref[0]
    nk = 3 * tq + n_ctx
    r = lax.broadcasted_iota(jnp.int32, (tq, nk), 0)
    c = lax.broadcasted_iota(jnp.int32, (tq, nk), 1)
    qpos = i * tq + r
    kpos = (i - 1) * tq + c
    ok = (c >= 3 * tq) | ((kpos >= 0) & (kpos < seq_len) & (jnp.abs(qpos - kpos) <= WINDOW))
    bias = jnp.where(ok, 0.0, NEG_BIG).astype(F32)
    _gqa_core(sink_ref, q_ref, kcat[...], vcat[...], bias, o_ref, tq)


def _win_lat_attn(q, k, v, kctx, vctx, sink):
    b, l, _ = q.shape
    tq = WINDOW
    nq = l // tq
    n_ctx = kctx.shape[1]
    kvw = WA_KV_HEADS * WA_HEAD_DIM
    prev = lambda bi, i: (bi, jnp.maximum(i - 1, 0), 0)
    cur = lambda bi, i: (bi, i, 0)
    nxt = lambda bi, i: (bi, jnp.minimum(i + 1, nq - 1), 0)
    kv_spec = lambda f: pl.BlockSpec((1, tq, kvw), f)
    return pl.pallas_call(
        functools.partial(_win_lat_kernel, tq=tq, seq_len=l, n_ctx=n_ctx),
        grid=(b, nq),
        in_specs=[
            pl.BlockSpec(memory_space=pltpu.SMEM),
            pl.BlockSpec((1, tq, D_MODEL), cur),
            kv_spec(prev), kv_spec(cur), kv_spec(nxt),
            kv_spec(prev), kv_spec(cur), kv_spec(nxt),
            pl.BlockSpec((1, n_ctx, kvw), lambda bi, i: (bi, 0, 0)),
            pl.BlockSpec((1, n_ctx, kvw), lambda bi, i: (bi, 0, 0)),
        ],
        out_specs=pl.BlockSpec((1, tq, D_MODEL), cur),
        out_shape=jax.ShapeDtypeStruct((b, l, D_MODEL), BF16),
        scratch_shapes=[pltpu.VMEM((3 * tq + n_ctx, kvw), BF16), pltpu.VMEM((3 * tq + n_ctx, kvw), BF16)],
        compiler_params=_cparams(("parallel", "parallel")),
        name="win_lat_attn",
    )(sink, q, k, k, k, v, v, v, kctx, vctx)


def _diff_kernel(lqk_ref, gsub_ref, q_ref, k_ref, v_ref, *rest, tq, kc, n_keys, has_ctx, lam_init):
    if has_ctx:
        kx_ref, vx_ref, o_ref, kall, vall, m_ref, acc_ref, s_a, s_b = rest
    else:
        o_ref, kall, vall, m_ref, acc_ref, s_a, s_b = rest
    hd = DA_HEAD_DIM
    lam = (jnp.exp(jnp.sum(lqk_ref[0:1, :] * lqk_ref[1:2, :], axis=-1, keepdims=True))
           - jnp.exp(jnp.sum(lqk_ref[2:3, :] * lqk_ref[3:4, :], axis=-1, keepdims=True)) + lam_init)
    hw = 2 * hd

    @pl.when(pl.program_id(2) == 0)
    def _():
        lat = k_ref.shape[1]
        kall[0:lat, :] = k_ref[0].astype(BF16)
        vall[0:lat, 0:hw] = v_ref[0].astype(BF16)
        if has_ctx:
            kall[lat:n_keys, :] = kx_ref[0]
            vall[lat:n_keys, 0:hw] = vx_ref[0]
        vall[:, hw:2 * hw] = jnp.ones((n_keys, hw), BF16)

    q = q_ref[0]
    lane = lax.broadcasted_iota(jnp.int32, (1, hw), 1)
    zero = jnp.zeros_like(q)
    q2 = jnp.concatenate([jnp.where(lane < hd, q, zero), jnp.where(lane >= hd, q, zero)], axis=0)

    n_chunks = n_keys // kc
    s_bufs = (s_a, s_b)
    s_bufs[0][...] = _dot_nt(q2, kall[0:kc, :])
    for j in range(n_chunks):
        if j + 1 < n_chunks:
            s_bufs[(j + 1) % 2][...] = _dot_nt(q2, kall[(j + 1) * kc:(j + 2) * kc, :])
        s = s_bufs[j % 2][...]
        m_chunk = jnp.max(s, axis=-1, keepdims=True)
        if j == 0:
            m_new = jnp.broadcast_to(m_chunk, m_ref.shape)
        else:
            m_old = m_ref[...]
            m_new = jnp.maximum(m_old, m_chunk)
            alpha = jnp.exp(m_old - m_new)
        p = jnp.exp(s - jnp.concatenate([m_new] * (kc // LANES), axis=1))
        pv = jnp.dot(p.astype(BF16), vall[j * kc:(j + 1) * kc, :], preferred_element_type=F32)
        if j == 0:
            acc_ref[...] = pv
        else:
            acc_ref[...] = jnp.concatenate([alpha, alpha], axis=1) * acc_ref[...] + pv
        if j + 1 < n_chunks:
            m_ref[...] = m_new
    acc = acc_ref[...]
    o2 = acc[:, 0:hw] / acc[:, hw:2 * hw]
    o = o2[0:tq] - lam * o2[tq:2 * tq]
    ms = jnp.mean(o * o, axis=-1, keepdims=True)
    o = (o * lax.rsqrt(ms + EPS)) * gsub_ref[...] * (1.0 - lam_init)
    o_ref[0] = o.astype(o_ref.dtype)


def _diff_attn(q, k, v, kctx, vctx, lqk, gsub, *, tq, kc, lam_init):
    b, l, _ = q.shape
    hw = 2 * DA_HEAD_DIM
    has_ctx = kctx is not None
    in_specs = [
        pl.BlockSpec((8, hw), lambda bi, h, i: (0, 0)),
        pl.BlockSpec((1, hw), lambda bi, h, i: (0, 0)),
        pl.BlockSpec((1, tq, hw), lambda bi, h, i: (bi, i, h)),
        pl.BlockSpec((1, l, hw), lambda bi, h, i: (bi, 0, h)),
        pl.BlockSpec((1, l, hw), lambda bi, h, i: (bi, 0, h)),
    ]
    args = [lqk, gsub, q, k, v]
    n_keys = l
    if has_ctx:
        n_ctx = kctx.shape[1]
        n_keys = l + n_ctx
        in_specs.append(pl.BlockSpec((1, n_ctx, hw), lambda bi, h, i: (bi, 0, h)))
        in_specs.append(pl.BlockSpec((1, n_ctx, hw), lambda bi, h, i: (bi, 0, h)))
        args += [kctx, vctx]
    assert n_keys % kc == 0 and kc % LANES == 0
    kern = functools.partial(_diff_kernel, tq=tq, kc=kc, n_keys=n_keys, has_ctx=has_ctx, lam_init=lam_init)
    return pl.pallas_call(
        kern,
        grid=(b, DA_HEADS, l // tq),
        in_specs=in_specs,
        out_specs=pl.BlockSpec((1, tq, hw), lambda bi, h, i: (bi, i, h)),
        out_shape=jax.ShapeDtypeStruct((b, l, D_MODEL), BF16),
        scratch_shapes=[pltpu.VMEM((n_keys, hw), BF16), pltpu.VMEM((n_keys, 2 * hw), BF16),
                        pltpu.VMEM((2 * tq, LANES), F32), pltpu.VMEM((2 * tq, 2 * hw), F32),
                        pltpu.VMEM((2 * tq, kc), F32), pltpu.VMEM((2 * tq, kc), F32)],
        compiler_params=_cparams(("parallel", "parallel", "arbitrary")),
        name="diff_lat_attn" if has_ctx else "diff_ctx_attn",
    )(*args)


def _gla_scan_kernel(q_ref, k_ref, v_ref, zl_ref, w2_ref, b2_ref, s0_ref, o_ref, sf_ref, s_scr,
                     *, blk, reverse, n_blocks):
    i = pl.program_id(1)
    ch = GLA_CHUNK
    nk = GLA_HEADS * GLA_DK

    @pl.when(i == 0)
    def _():
        s_scr[...] = s0_ref[0]

    r = lax.broadcasted_iota(jnp.int32, (ch, ch), 0)
    c = lax.broadcasted_iota(jnp.int32, (ch, ch), 1)
    keep = (r <= c) if reverse else (r >= c)
    tri = keep.astype(BF16)
    n_sub = blk // ch
    order = range(n_sub - 1, -1, -1) if reverse else range(n_sub)
    edge = 0 if reverse else ch - 1
    for sub in order:
        rows = slice(sub * ch, (sub + 1) * ch)
        z = jnp.dot(zl_ref[0, rows, :].astype(BF16), w2_ref[...], preferred_element_type=F32) + b2_ref[...]
        logg = (jnp.minimum(z, 0.0) - jnp.log1p(jnp.exp(-jnp.abs(z)))) / GLA_TAU
        cum = _exact_dot_left01(tri, logg)
        blast = cum[edge:edge + 1, :]
        e_pos = jnp.exp(cum)
        qd = (q_ref[0, rows, :] * (GLA_DK ** -0.5)) * e_pos
        kd = k_ref[0, rows, :] * jnp.exp(-cum)
        kr = k_ref[0, rows, :] * jnp.exp(blast - cum)
        e_last = jnp.exp(blast)
        for h in range(GLA_HEADS):
            ks = slice(h * GLA_DK, (h + 1) * GLA_DK)
            vs = slice(h * GLA_DV, (h + 1) * GLA_DV)
            qd_h = qd[:, ks].astype(BF16)
            v_h = v_ref[0, rows, vs].astype(BF16)
            att = jnp.where(keep, _dot_nt(qd_h, kd[:, ks].astype(BF16)), 0.0)
            s_h = s_scr[h]
            o_h = (jnp.dot(att.astype(BF16), v_h, preferred_element_type=F32)
                   + jnp.dot(qd_h, s_h.astype(BF16), preferred_element_type=F32))
            o_ref[0, rows, vs] = o_h
            kr_t = jnp.transpose(kr[:, ks]).astype(BF16)
            decay = jnp.transpose(jnp.broadcast_to(e_last[:, ks], (GLA_DK, GLA_DK)))
            decay = jnp.concatenate([decay] * (GLA_DV // GLA_DK), axis=1)
            s_scr[h] = s_h * decay + jnp.dot(kr_t, v_h, preferred_element_type=F32)

    @pl.when(i == n_blocks - 1)
    def _():
        sf_ref[0] = s_scr[...]


def _gla_scan(q, k, v, zl, w2, b2, s0, *, blk, reverse):
    b, l, nk = q.shape
    nv = v.shape[-1]
    n_blocks = l // blk
    blk_idx = (lambda bi, i: (bi, n_blocks - 1 - i, 0)) if reverse else (lambda bi, i: (bi, i, 0))
    state_spec = pl.BlockSpec((1, GLA_HEADS, GLA_DK, GLA_DV), lambda bi, i: (bi, 0, 0, 0))
    return pl.pallas_call(
        functools.partial(_gla_scan_kernel, blk=blk, reverse=reverse, n_blocks=n_blocks),
        grid=(b, n_blocks),
        in_specs=[
            pl.BlockSpec((1, blk, nk), blk_idx),
            pl.BlockSpec((1, blk, nk), blk_idx),
            pl.BlockSpec((1, blk, nv), blk_idx),
            pl.BlockSpec((1, blk, LANES), blk_idx),
            pl.BlockSpec((LANES, nk), lambda bi, i: (0, 0)),
            pl.BlockSpec((1, nk), lambda bi, i: (0, 0)),
            state_spec,
        ],
        out_specs=[pl.BlockSpec((1, blk, nv), blk_idx), state_spec],
        out_shape=[jax.ShapeDtypeStruct((b, l, nv), F32),
                   jax.ShapeDtypeStruct((b, GLA_HEADS, GLA_DK, GLA_DV), F32)],
        scratch_shapes=[pltpu.VMEM((GLA_HEADS, GLA_DK, GLA_DV), F32)],
        compiler_params=_cparams(("parallel", "arbitrary")),
        name="gla_scan_bwd" if reverse else "gla_scan_fwd",
    )(q, k, v, zl, w2, b2, s0)


def _gla_out_kernel(of_ref, ob_ref, r_ref, gh_ref, w_ref, x_ref, mod_ref, out_ref):
    o = of_ref[0] + ob_ref[0]
    parts = []
    for h in range(GLA_HEADS):
        oh = o[:, h * GLA_DV:(h + 1) * GLA_DV]
        ms = jnp.mean(oh * oh, axis=-1, keepdims=True)
        parts.append((oh * lax.rsqrt(ms + EPS)) * gh_ref[...])
    y = jnp.concatenate(parts, axis=1) * _silu(r_ref[0])
    acc = jnp.dot(y.astype(BF16), w_ref[...], preferred_element_type=F32)
    out_ref[0] = x_ref[0] + mod_ref[0, 2:3, :] * acc


def _gla_out(o_f, o_b, r, g_head, w, x, mod, *, tm):
    b, l, nv = o_f.shape
    row = lambda bi, i: (bi, i, 0)
    return pl.pallas_call(
        _gla_out_kernel,
        grid=(b, l // tm),
        in_specs=[
            pl.BlockSpec((1, tm, nv), row),
            pl.BlockSpec((1, tm, nv), row),
            pl.BlockSpec((1, tm, nv), row),
            pl.BlockSpec((1, GLA_DV), lambda bi, i: (0, 0)),
            pl.BlockSpec((nv, D_MODEL), lambda bi, i: (0, 0)),
            pl.BlockSpec((1, tm, D_MODEL), row),
            _mod_spec(mod),
        ],
        out_specs=pl.BlockSpec((1, tm, D_MODEL), row),
        out_shape=jax.ShapeDtypeStruct((b, l, D_MODEL), F32),
        compiler_params=_cparams(("parallel", "parallel")),
        name="gla_out",
    )(o_f, o_b, r, g_head.reshape(1, GLA_DV), w, x, mod)


def _ssd_scan_kernel(x_ref, b_ref, c_ref, dt_ref, bias_ref, a_ref, dsk_ref, s0_ref, y_ref, sf_ref, s_scr,
                     *, reverse, n_chunks, lane_off, add_skip):
    i = pl.program_id(1)
    ch = SSD_CHUNK
    p = SSD_HEAD_DIM
    hpg = SSD_HEADS // SSD_GROUPS

    @pl.when(i == 0)
    def _():
        s_scr[...] = s0_ref[0]

    r = lax.broadcasted_iota(jnp.int32, (ch, ch), 0)
    c = lax.broadcasted_iota(jnp.int32, (ch, ch), 1)
    keep = (r <= c) if reverse else (r >= c)
    tri = keep.astype(BF16)
    edge = 0 if reverse else ch - 1

    dt = _softplus(dt_ref[0] + bias_ref[...])
    la = dt * a_ref[...]
    cum = _exact_dot_left01(tri, la)
    cum_t = jnp.transpose(cum)
    dt_t = jnp.transpose(dt)
    lane = lax.broadcasted_iota(jnp.int32, (1, LANES), 1)
    lo_half = lane < p

    for g in range(SSD_GROUPS):
        bg = b_ref[0, :, g * SSD_STATE:(g + 1) * SSD_STATE].astype(BF16)
        cg = c_ref[0, :, g * SSD_STATE:(g + 1) * SSD_STATE].astype(BF16)
        cb = _dot_nt(cg, bg)
        gs = slice(g * hpg * p, (g + 1) * hpg * p)
        s_g = s_scr[gs, :]
        y_inter = _dot_nt(cg, s_g.astype(BF16))
        xs_parts = []
        for pair in range(hpg // 2):
            cols = slice((g * hpg + 2 * pair) * p, (g * hpg + 2 * pair + 2) * p)
            x_pair = x_ref[0, :, cols]
            ws = []
            e_is = []
            te_s = []
            for t in range(2):
                hl = lane_off + g * hpg + 2 * pair + t
                cum_i = jnp.broadcast_to(cum[:, hl:hl + 1], (ch, ch))
                cum_j = cum_t[hl:hl + 1, :]
                seg = cum_i - cum_j
                decay = jnp.where(keep, jnp.exp(jnp.where(keep, seg, 0.0)), 0.0)
                ws.append((cb * decay * dt_t[hl:hl + 1, :]).astype(BF16))
                e_is.append(jnp.exp(cum_i))
                tot = cum_i[edge:edge + 1, :]
                te_s.append(jnp.exp(tot - cum_i) * jnp.broadcast_to(dt[:, hl:hl + 1], (ch, ch)))
            xb = x_pair.astype(BF16)
            zero = jnp.zeros_like(xb)
            x_bd = jnp.concatenate([jnp.where(lo_half, xb, zero), jnp.where(lo_half, zero, xb)], axis=0)
            y_pair = jnp.dot(jnp.concatenate(ws, axis=1), x_bd, preferred_element_type=F32)
            y_pair = y_pair + y_inter[:, 2 * pair * p:(2 * pair + 2) * p] * jnp.where(lo_half, e_is[0], e_is[1])
            if add_skip:
                hl0 = g * hpg + 2 * pair
                dsk = jnp.where(lo_half, dsk_ref[:, hl0:hl0 + 1], dsk_ref[:, hl0 + 1:hl0 + 2])
                y_pair = y_pair + x_pair * dsk
            y_ref[0, :, cols] = y_pair
            xs_parts.append(x_pair * jnp.where(lo_half, te_s[0], te_s[1]))
        xs = jnp.concatenate(xs_parts, axis=1)
        ds = jnp.dot(jnp.transpose(xs).astype(BF16), bg, preferred_element_type=F32)
        for hh in range(hpg):
            hl = lane_off + g * hpg + hh
            tot = jnp.exp(cum_t[hl:hl + 1, edge:edge + 1])
            rs = slice((g * hpg + hh) * p, (g * hpg + hh + 1) * p)
            s_scr[rs, :] = s_scr[rs, :] * tot + ds[hh * p:(hh + 1) * p, :]

    @pl.when(i == n_chunks - 1)
    def _():
        sf_ref[0] = s_scr[...]


def _ssd_scan(xbc, dt, bias, a, dskip, s0, *, reverse, add_skip):
    b, l, _ = xbc.shape
    ch = SSD_CHUNK
    n_chunks = l // ch
    gn = SSD_GROUPS * SSD_STATE
    rows = SSD_HEADS * SSD_HEAD_DIM

    def at(col):
        if reverse:
            return lambda bi, i: (bi, n_chunks - 1 - i, col)
        return lambda bi, i: (bi, i, col)

    vec = pl.BlockSpec((1, LANES), lambda bi, i: (0, 0))
    state_spec = pl.BlockSpec((1, rows, SSD_STATE), lambda bi, i: (bi, 0, 0))
    kern = functools.partial(_ssd_scan_kernel, reverse=reverse, n_chunks=n_chunks,
                             lane_off=SSD_HEADS if reverse else 0, add_skip=add_skip)
    return pl.pallas_call(
        kern,
        grid=(b, n_chunks),
        in_specs=[
            pl.BlockSpec((1, ch, SSD_D_INNER), at(0)),
            pl.BlockSpec((1, ch, gn), at(SSD_D_INNER // gn)),
            pl.BlockSpec((1, ch, gn), at(SSD_D_INNER // gn + 1)),
            pl.BlockSpec((1, ch, LANES), at(0)),
            vec, vec, vec,
            state_spec,
        ],
        out_specs=[pl.BlockSpec((1, ch, SSD_D_INNER), at(0)), state_spec],
        out_shape=[jax.ShapeDtypeStruct((b, l, SSD_D_INNER), F32),
                   jax.ShapeDtypeStruct((b, rows, SSD_STATE), F32)],
        scratch_shapes=[pltpu.VMEM((rows, SSD_STATE), F32)],
        compiler_params=_cparams(("parallel", "arbitrary")),
        name="ssd_scan_bwd" if reverse else "ssd_scan_fwd",
    )(xbc, xbc, xbc, dt, bias, a, dskip, s0)


def _ssd_out_kernel(yf_ref, yb_ref, z_ref, gn_ref, w_ref, x_ref, mod_ref, out_ref):
    y = (yf_ref[0] + yb_ref[0]) * _silu(z_ref[0])
    ms = jnp.mean(y * y, axis=-1, keepdims=True)
    y = (y * lax.rsqrt(ms + EPS)) * gn_ref[...]
    acc = jnp.dot(y.astype(BF16), w_ref[...], preferred_element_type=F32)
    out_ref[0] = x_ref[0] + mod_ref[0, 2:3, :] * acc


def _ssd_out(y_f, y_b, z, g_norm, w, x, mod, *, tm):
    b, l, di = y_f.shape
    row = lambda bi, i: (bi, i, 0)
    return pl.pallas_call(
        _ssd_out_kernel,
        grid=(b, l // tm),
        in_specs=[
            pl.BlockSpec((1, tm, di), row),
            pl.BlockSpec((1, tm, di), row),
            pl.BlockSpec((1, tm, di), row),
            pl.BlockSpec((1, di), lambda bi, i: (0, 0)),
            pl.BlockSpec((di, D_MODEL), lambda bi, i: (0, 0)),
            pl.BlockSpec((1, tm, D_MODEL), row),
            _mod_spec(mod),
        ],
        out_specs=pl.BlockSpec((1, tm, D_MODEL), row),
        out_shape=jax.ShapeDtypeStruct((b, l, D_MODEL), F32),
        compiler_params=_cparams(("parallel", "parallel")),
        name="ssd_out",
    )(y_f, y_b, z, g_norm.reshape(1, di), w, x, mod)


def _rope_tables(n_tokens, dim):
    rows = n_tokens // GRID_W
    row = jnp.repeat(jnp.arange(rows, dtype=F32), GRID_W)
    col = jnp.tile(jnp.arange(GRID_W, dtype=F32), rows)
    axis_dim = dim // 2
    inv = ROPE_BASE ** (-jnp.arange(0, axis_dim, 2, dtype=F32) / axis_dim)
    ar = row[:, None] * inv
    ac = col[:, None] * inv
    cos = jnp.concatenate([jnp.cos(ar), jnp.cos(ar), jnp.cos(ac), jnp.cos(ac)], axis=1)
    sin = jnp.concatenate([-jnp.sin(ar), jnp.sin(ar), -jnp.sin(ac), jnp.sin(ac)], axis=1)
    reps = LANES // dim
    return jnp.tile(cos, (1, reps)), jnp.tile(sin, (1, reps))


def _pad_cols(w, width):
    return jnp.pad(w, ((0, 0), (0, width - w.shape[1])))


def _conv_pack(conv_w, conv_b):
    return jnp.concatenate([conv_w, conv_b[None], jnp.zeros((4, conv_w.shape[1]), F32)], axis=0)


def _tm_for(l):
    return min(l, 512)


def _key_chunk(n_keys):
    for kc in (768, 512, 256):
        if n_keys % kc == 0:
            return kc
    return n_keys


def kernel(x_prompt, x_sample, c, c_ctx, cache_win_k, cache_win_v, state_gla_fwd, state_gla_bwd, cache_diff_k, cache_diff_v, state_ssd_fwd, state_ssd_bwd, ada_w, ada_b, norm_mix, norm_ffn, ffn_w_up, ffn_conv_w, ffn_conv_b, ffn_w_down, final_norm, win_w_qkv, win_w_o, win_sink, gla_w_qkvr, gla_w_gf1, gla_w_gf2, gla_b_gf, gla_w_gb1, gla_w_gb2, gla_b_gb, gla_norm, gla_w_o, diff_w_qkv, diff_lq1, diff_lk1, diff_lq2, diff_lk2, diff_norm, diff_w_o, ssd_w_in, ssd_conv_w, ssd_conv_b, ssd_a_log_f, ssd_a_log_b, ssd_dt_bias_f, ssd_dt_bias_b, ssd_d, ssd_norm, ssd_w_out):
    xp, xs = x_prompt, x_sample
    bp, lp, d = xp.shape
    bs, ls, _ = xs.shape
    tmp, tms = _tm_for(lp), _tm_for(ls)

    n_cond = 1 + bs
    cond_rows = -(-n_cond // SUBLANES) * SUBLANES
    cond = jnp.concatenate([c_ctx[None], c, jnp.zeros((cond_rows - n_cond, d), F32)], axis=0)
    mods = _ada_call(cond, ada_w, ada_b).reshape(DEPTH, cond_rows, 6, d)
    mods = jnp.pad(mods, ((0, 0), (0, 0), (0, 2), (0, 0)))

    outs = {}
    for i in range(DEPTH):
        kind, j = i % 4, i // 4
        mod_p = mods[i, 0:1]
        mod_s = mods[i, 1:1 + bs]
        if kind == 0:
            nq = WA_HEADS * WA_HEAD_DIM
            nkv = WA_KV_HEADS * WA_HEAD_DIM
            perm = jnp.arange(nq).reshape(WA_KV_HEADS, WA_GROUP, WA_HEAD_DIM).transpose(1, 0, 2).reshape(-1)
            wq = win_w_qkv[j][:, :nq][:, perm]
            w = jnp.concatenate([wq, win_w_qkv[j][:, nq:]], axis=1).astype(BF16)
            w_o = win_w_o[j][perm, :].astype(BF16)
            sink = win_sink[j]
            qscale = WA_HEAD_DIM ** -0.5
            q, k, v = _fused_proj(xp, norm_mix[i], mod_p, w,
                                  [(nq, "plain", qscale), (nkv, "plain", 1.0), (nkv, "plain", 1.0)],
                                  [BF16, F32, F32], tm=tmp, name="win_proj_ctx")
            outs["win_k"] = k.reshape(bp, 1, lp, WA_KV_HEADS, WA_HEAD_DIM)
            outs["win_v"] = v.reshape(bp, 1, lp, WA_KV_HEADS, WA_HEAD_DIM)
            o = _win_ctx_attn(q, k, v, sink, tq=min(lp, 128))
            xp = _out_proj(o, w_o, xp, mod_p, tm=tmp, name="win_out_ctx")
            rope = _rope_tables(ls, WA_HEAD_DIM)
            q, k, v = _fused_proj(xs, norm_mix[i], mod_s, w,
                                  [(nq, "rope", qscale), (nkv, "rope", 1.0), (nkv, "plain", 1.0)],
                                  [BF16, BF16, BF16], tm=tms, rope=rope, name="win_proj_lat")
            n_ctx = cache_win_k.shape[2]
            kctx = cache_win_k[:, j].reshape(bs, n_ctx, nkv).astype(BF16)
            vctx = cache_win_v[:, j].reshape(bs, n_ctx, nkv).astype(BF16)
            o = _win_lat_attn(q, k, v, kctx, vctx, sink)
            xs = _out_proj(o, w_o, xs, mod_s, tm=tms, name="win_out_lat")
        elif kind == 1:
            nk = GLA_HEADS * GLA_DK
            nv = GLA_HEADS * GLA_DV
            w1 = _pad_cols(jnp.concatenate([gla_w_gf1[j], gla_w_gb1[j]], axis=1), LANES)
            w = jnp.concatenate([gla_w_qkvr[j], w1], axis=1).astype(BF16)
            segs = [(nk, "plain", 1.0), (nk, "plain", 1.0), (nv, "plain", 1.0), (nv, "plain", 1.0),
                    (LANES, "plain", 1.0)]
            zrows = jnp.zeros((LANES - 2 * GLA_RANK, nk), F32)
            w2_f = jnp.concatenate([gla_w_gf2[j], jnp.zeros((GLA_RANK, nk), F32), zrows], axis=0).astype(BF16)
            w2_b = jnp.concatenate([jnp.zeros((GLA_RANK, nk), F32), gla_w_gb2[j], zrows], axis=0).astype(BF16)
            b2_f = gla_b_gf[j].reshape(1, nk)
            b2_b = gla_b_gb[j].reshape(1, nk)
            w_o = gla_w_o[j].astype(BF16)
            for stream in ("p", "s"):
                if stream == "p":
                    x, mod, tm, bsz = xp, mod_p, tmp, bp
                    s0_f = jnp.zeros((bp, GLA_HEADS, GLA_DK, GLA_DV), F32)
                    s0_b = s0_f
                else:
                    x, mod, tm, bsz = xs, mod_s, tms, bs
                    s0_f = state_gla_fwd[:, j]
                    s0_b = state_gla_bwd[:, j]
                q, k, v, r, zl = _fused_proj(x, norm_mix[i], mod, w, segs, [F32] * 5, tm=tm,
                                             name="gla_proj_" + stream)
                blk = min(x.shape[1], 256)
                o_f, s_f = _gla_scan(q, k, v, zl, w2_f, b2_f, s0_f, blk=blk, reverse=False)
                o_b, s_b = _gla_scan(q, k, v, zl, w2_b, b2_b, s0_b, blk=blk, reverse=True)
                x = _gla_out(o_f, o_b, r, gla_norm[j], w_o, x, mod, tm=tm)
                if stream == "p":
                    xp = x
                    outs["gla_f"] = s_f[:, None]
                    outs["gla_b"] = s_b[:, None]
                else:
                    xs = x
        elif kind == 2:
            lam_init = 0.8 - 0.6 * math.exp(-0.3 * i)
            nh = DA_HEADS * 2 * DA_HEAD_DIM
            w = diff_w_qkv[j].astype(BF16)
            w_o = diff_w_o[j].astype(BF16)
            qscale = DA_HEAD_DIM ** -0.5
            lqk = jnp.stack([diff_lq1[j], diff_lk1[j], diff_lq2[j], diff_lk2[j]], axis=0)
            lqk = jnp.pad(lqk, ((0, 4), (0, 2 * DA_HEAD_DIM - lqk.shape[1])))
            gsub = diff_norm[j].reshape(1, 2 * DA_HEAD_DIM)
            q, k, v = _fused_proj(xp, norm_mix[i], mod_p, w,
                                  [(nh, "plain", qscale), (nh, "plain", 1.0), (nh, "plain", 1.0)],
                                  [BF16, F32, F32], tm=tmp, name="diff_proj_ctx")
            outs["diff_k"] = k.reshape(bp, 1, lp, DA_HEADS, 2, DA_HEAD_DIM)
            outs["diff_v"] = v.reshape(bp, 1, lp, DA_HEADS, 2 * DA_HEAD_DIM)
            o = _diff_attn(q, k, v, None, None, lqk, gsub, tq=min(lp, 256), kc=_key_chunk(lp), lam_init=lam_init)
            xp = _out_proj(o, w_o, xp, mod_p, tm=tmp, name="diff_out_ctx")
            rope = _rope_tables(ls, DA_HEAD_DIM)
            q, k, v = _fused_proj(xs, norm_mix[i], mod_s, w,
                                  [(nh, "rope", qscale), (nh, "rope", 1.0), (nh, "plain", 1.0)],
                                  [BF16, BF16, BF16], tm=tms, rope=rope, name="diff_proj_lat")
            n_ctx = cache_diff_k.shape[2]
            kctx = cache_diff_k[:, j].reshape(bs, n_ctx, nh).astype(BF16)
            vctx = cache_diff_v[:, j].reshape(bs, n_ctx, nh).astype(BF16)
            o = _diff_attn(q, k, v, kctx, vctx, lqk, gsub, tq=min(ls, 256), kc=_key_chunk(ls + n_ctx),
                           lam_init=lam_init)
            xs = _out_proj(o, w_o, xs, mod_s, tm=tms, name="diff_out_lat")
        else:
            gn = SSD_GROUPS * SSD_STATE
            nxbc = SSD_D_INNER + 2 * gn
            w_in = ssd_w_in[j]
            w = jnp.concatenate([w_in[:, :SSD_D_INNER + nxbc], _pad_cols(w_in[:, SSD_D_INNER + nxbc:], LANES)],
                                axis=1).astype(BF16)
            segs = [(SSD_D_INNER, "plain", 1.0), (nxbc, "conv_silu", 1.0), (LANES, "plain", 1.0)]
            cw = _conv_pack(ssd_conv_w[j], ssd_conv_b[j])
            zpad = jnp.zeros((LANES - 2 * SSD_HEADS,), F32)
            zh = jnp.zeros((SSD_HEADS,), F32)
            bias = jnp.concatenate([ssd_dt_bias_f[j], ssd_dt_bias_b[j], zpad]).reshape(1, LANES)
            a_f = jnp.concatenate([-jnp.exp(ssd_a_log_f[j]), zh, zpad]).reshape(1, LANES)
            a_b = jnp.concatenate([zh, -jnp.exp(ssd_a_log_b[j]), zpad]).reshape(1, LANES)
            dsk = jnp.concatenate([ssd_d[j], zh, zpad]).reshape(1, LANES)
            w_out = ssd_w_out[j].astype(BF16)
            rows = SSD_HEADS * SSD_HEAD_DIM
            for stream in ("p", "s"):
                if stream == "p":
                    x, mod, tm, bsz = xp, mod_p, tmp, bp
                    s0_f = jnp.zeros((bp, rows, SSD_STATE), F32)
                    s0_b = s0_f
                else:
                    x, mod, tm, bsz = xs, mod_s, tms, bs
                    s0_f = state_ssd_fwd[:, j].reshape(bs, rows, SSD_STATE)
                    s0_b = state_ssd_bwd[:, j].reshape(bs, rows, SSD_STATE)
                z, xbc, dt = _fused_proj(x, norm_mix[i], mod, w, segs, [F32] * 3, tm=min(tm, 256), conv_w=cw,
                                         name="ssd_proj_" + stream)
                y_f, s_f = _ssd_scan(xbc, dt, bias, a_f, dsk, s0_f, reverse=False, add_skip=True)
                y_b, s_b = _ssd_scan(xbc, dt, bias, a_b, dsk, s0_b, reverse=True, add_skip=False)
                x = _ssd_out(y_f, y_b, z, ssd_norm[j], w_out, x, mod, tm=tm)
                if stream == "p":
                    xp = x
                    outs["ssd_f"] = s_f.reshape(bp, 1, SSD_HEADS, SSD_HEAD_DIM, SSD_STATE)
                    outs["ssd_b"] = s_b.reshape(bp, 1, SSD_HEADS, SSD_HEAD_DIM, SSD_STATE)
                else:
                    xs = x

        w_up = ffn_w_up[i].astype(BF16)
        cw = _conv_pack(ffn_conv_w[i], ffn_conv_b[i])
        wd = ffn_w_down[i].astype(BF16)
        last = i == DEPTH - 1
        xp = _ffn(xp, norm_ffn[i], mod_p, w_up, cw, wd, final_norm, tm=tmp, final_norm=last, name="ffn_p")
        xs = _ffn(xs, norm_ffn[i], mod_s, w_up, cw, wd, final_norm, tm=tms, final_norm=last, name="ffn_s")

    return (xp, xs, outs["win_k"], outs["win_v"], outs["gla_f"], outs["gla_b"],
            outs["diff_k"], outs["diff_v"], outs["ssd_f"], outs["ssd_b"])
```

```python
import functools
import math

import jax
import jax.numpy as jnp
from jax import lax
from jax.experimental import pallas as pl
from jax.experimental.pallas import tpu as pltpu

F32 = jnp.float32
BF16 = jnp.bfloat16

D_MODEL = 1024
DEPTH = 4
GRID_W = 64
EPS = 1e-6
ROPE_BASE = 10000.0

WA_HEADS = 16
WA_KV_HEADS = 4
WA_GROUP = 4
WA_HEAD_DIM = 64
WINDOW = 128

GLA_HEADS = 4
GLA_DK = 128
GLA_DV = 256
GLA_RANK = 16
GLA_TAU = 16.0
GLA_CHUNK = 64
GLA_CUM_ROWS = 256

DA_HEADS = 8
DA_HEAD_DIM = 64

SSD_D_INNER = 2048
SSD_HEAD_DIM = 64
SSD_HEADS = 32
SSD_GROUPS = 4
SSD_STATE = 128
SSD_CHUNK = 128

D_FF = 2816
FF_CHUNK = 256
FF_GROUP = 4

VMEM_LIMIT_BYTES = 56 * 1024 * 1024
SUBLANES = 8
LANES = 128
NEG_BIG = -1e30
LOG2_E = 1.4426950408889634


def _cparams(sem):
    return pltpu.CompilerParams(dimension_semantics=sem, vmem_limit_bytes=VMEM_LIMIT_BYTES)


def _sigmoid(x):
    return 1.0 / (1.0 + jnp.exp(-x))


def _silu(x):
    return x * _sigmoid(x)


def _softplus(x):
    return jnp.maximum(x, 0.0) + jnp.log1p(jnp.exp(-jnp.abs(x)))


def _norm_mod(x, gamma, shift, scale):
    ms = jnp.mean(x * x, axis=-1, keepdims=True)
    y = (x * lax.rsqrt(ms + EPS)) * gamma
    return y * (1.0 + scale) + shift


def _split3(x):
    hi = x.astype(BF16)
    r1 = x - hi.astype(F32)
    mid = r1.astype(BF16)
    lo = (r1 - mid.astype(F32)).astype(BF16)
    return hi, mid, lo


def _exact_dot_left01(m01, x):
    hi, mid, lo = _split3(x)
    d = lambda p: jnp.dot(m01, p, preferred_element_type=F32)
    return d(hi) + d(mid) + d(lo)


def _dot_nt(a, b):
    return lax.dot_general(a, b, (((1,), (1,)), ((), ())), preferred_element_type=F32)


def _ada_kernel(c_ref, w_ref, b_ref, o_ref):
    h = _silu(c_ref[...]).astype(BF16)
    o_ref[0] = jnp.dot(h, w_ref[0].astype(BF16), preferred_element_type=F32) + b_ref[0]


def _ada_call(cond, ada_w, ada_b):
    rows = cond.shape[0]
    n = ada_w.shape[-1]
    tn = 1536
    return pl.pallas_call(
        _ada_kernel,
        grid=(DEPTH, n // tn),
        in_specs=[
            pl.BlockSpec((rows, D_MODEL), lambda l, j: (0, 0)),
            pl.BlockSpec((1, D_MODEL, tn), lambda l, j: (l, 0, j)),
            pl.BlockSpec((1, 1, tn), lambda l, j: (l, 0, j)),
        ],
        out_specs=pl.BlockSpec((1, rows, tn), lambda l, j: (l, 0, j)),
        out_shape=jax.ShapeDtypeStruct((DEPTH, rows, n), F32),
        compiler_params=_cparams(("parallel", "parallel")),
        name="ada_mod",
    )(cond, ada_w, ada_b.reshape(DEPTH, 1, n))


def _rope_apply(y, cos, sin):
    lane = lax.broadcasted_iota(jnp.int32, (1, LANES), 1)
    first = (lane % 32) < 16
    partner = jnp.where(first, pltpu.roll(y, LANES - 16, 1), pltpu.roll(y, 16, 1))
    return y * cos + partner * sin


def _fused_proj_kernel(*refs, tm, segs, has_conv, has_rope, n_tiles, chunk):
    it = iter(refs)
    x_ref = next(it)
    if has_conv:
        xp_ref = next(it)
        xn_ref = next(it)
    g_ref = next(it)
    mod_ref = next(it)
    w_ref = next(it)
    if has_rope:
        cos_ref = next(it)
        sin_ref = next(it)
    if has_conv:
        cw_ref = next(it)
    out_refs = [next(it) for _ in segs]
    h_ref = next(it)

    i = pl.program_id(1)
    gamma = g_ref[...]
    shift = mod_ref[0, 0:1, :]
    scale = mod_ref[0, 1:2, :]
    off = SUBLANES if has_conv else 0
    h_ref[off:off + tm, :] = _norm_mod(x_ref[0], gamma, shift, scale).astype(BF16)
    if has_conv:
        hp = _norm_mod(xp_ref[0], gamma, shift, scale)
        hn = _norm_mod(xn_ref[0], gamma, shift, scale)
        h_ref[0:SUBLANES, :] = jnp.where(i > 0, hp, 0.0).astype(BF16)
        h_ref[off + tm:off + tm + SUBLANES, :] = jnp.where(i < n_tiles - 1, hn, 0.0).astype(BF16)

    col = 0
    conv_col = 0
    for seg, o_ref in zip(segs, out_refs):
        width, epi, qscale = seg
        for c0 in range(0, width, chunk):
            wc = min(chunk, width - c0)
            w = w_ref[:, col + c0:col + c0 + wc]
            if epi == "conv_silu":
                u = jnp.dot(h_ref[...], w, preferred_element_type=F32)
                rows = tm + 2 * SUBLANES
                up = pltpu.roll(u, 1, 0)[off:off + tm]
                un = pltpu.roll(u, rows - 1, 0)[off:off + tm]
                uc = u[off:off + tm]
                cw = cw_ref[:, conv_col + c0:conv_col + c0 + wc]
                y = cw[0:1] * up + cw[1:2] * uc + cw[2:3] * un + cw[3:4]
                y = _silu(y)
            else:
                y = jnp.dot(h_ref[off:off + tm, :], w, preferred_element_type=F32)
                if qscale != 1.0:
                    y = y * qscale
                if epi == "rope":
                    cos = cos_ref[...]
                    sin = sin_ref[...]
                    y = jnp.concatenate(
                        [_rope_apply(y[:, k:k + LANES], cos, sin) for k in range(0, wc, LANES)], axis=1)
            o_ref[0, :, c0:c0 + wc] = y.astype(o_ref.dtype)
        col += width
        if epi == "conv_silu":
            conv_col += width


def _fused_proj(x, gamma, mod, w, segs, out_dtypes, *, tm, rope=None, conv_w=None, name):
    b, l, d = x.shape
    n_tiles = l // tm
    has_conv = conv_w is not None
    has_rope = rope is not None
    per_batch = mod.shape[0] > 1
    n_total = w.shape[1]
    bpt = tm // SUBLANES
    nblk8 = l // SUBLANES

    in_specs = [pl.BlockSpec((1, tm, d), lambda bi, i: (bi, i, 0))]
    args = [x]
    if has_conv:
        in_specs.append(pl.BlockSpec((1, SUBLANES, d), lambda bi, i: (bi, jnp.maximum(i * bpt - 1, 0), 0)))
        in_specs.append(pl.BlockSpec((1, SUBLANES, d), lambda bi, i: (bi, jnp.minimum((i + 1) * bpt, nblk8 - 1), 0)))
        args += [x, x]
    in_specs.append(pl.BlockSpec((1, d), lambda bi, i: (0, 0)))
    args.append(gamma.reshape(1, d))
    in_specs.append(pl.BlockSpec((1, 8, d), (lambda bi, i: (bi, 0, 0)) if per_batch else (lambda bi, i: (0, 0, 0))))
    args.append(mod)
    in_specs.append(pl.BlockSpec((d, n_total), lambda bi, i: (0, 0)))
    args.append(w)
    if has_rope:
        in_specs.append(pl.BlockSpec((tm, LANES), lambda bi, i: (i, 0)))
        in_specs.append(pl.BlockSpec((tm, LANES), lambda bi, i: (i, 0)))
        args += [rope[0], rope[1]]
    if has_conv:
        in_specs.append(pl.BlockSpec(conv_w.shape, lambda bi, i: (0, 0)))
        args.append(conv_w)

    out_specs = [pl.BlockSpec((1, tm, s[0]), lambda bi, i: (bi, i, 0)) for s in segs]
    out_shape = [jax.ShapeDtypeStruct((b, l, s[0]), dt) for s, dt in zip(segs, out_dtypes)]
    hrows = tm + (2 * SUBLANES if has_conv else 0)
    kern = functools.partial(_fused_proj_kernel, tm=tm, segs=tuple(segs), has_conv=has_conv,
                             has_rope=has_rope, n_tiles=n_tiles, chunk=512)
    return pl.pallas_call(
        kern,
        grid=(b, n_tiles),
        in_specs=in_specs,
        out_specs=out_specs,
        out_shape=out_shape,
        scratch_shapes=[pltpu.VMEM((hrows, d), BF16)],
        compiler_params=_cparams(("parallel", "parallel")),
        name=name,
    )(*args)


def _out_proj_kernel(o_ref, w_ref, x_ref, mod_ref, out_ref, *, gate_row):
    acc = jnp.dot(o_ref[0].astype(BF16), w_ref[...], preferred_element_type=F32)
    out_ref[0] = x_ref[0] + mod_ref[0, gate_row:gate_row + 1, :] * acc


def _mod_spec(mod):
    if mod.shape[0] > 1:
        return pl.BlockSpec((1, 8, D_MODEL), lambda bi, i: (bi, 0, 0))
    return pl.BlockSpec((1, 8, D_MODEL), lambda bi, i: (0, 0, 0))


def _out_proj(o, w, x, mod, *, tm, name):
    b, l, k = o.shape
    return pl.pallas_call(
        functools.partial(_out_proj_kernel, gate_row=2),
        grid=(b, l // tm),
        in_specs=[
            pl.BlockSpec((1, tm, k), lambda bi, i: (bi, i, 0)),
            pl.BlockSpec((k, D_MODEL), lambda bi, i: (0, 0)),
            pl.BlockSpec((1, tm, D_MODEL), lambda bi, i: (bi, i, 0)),
            _mod_spec(mod),
        ],
        out_specs=pl.BlockSpec((1, tm, D_MODEL), lambda bi, i: (bi, i, 0)),
        out_shape=jax.ShapeDtypeStruct((b, l, D_MODEL), F32),
        compiler_params=_cparams(("parallel", "parallel")),
        name=name,
    )(o, w, x, mod)


def _ffn_kernel(x_ref, xp_ref, xn_ref, g_ref, mod_ref, wu_ref, cw_ref, wd_ref, fg_ref,
                out_ref, h_ref, hnat_ref, acc_ref, act_ref, *, tm, n_tiles, final_norm):
    i = pl.program_id(1)
    gamma = g_ref[...]
    shift = mod_ref[0, 3:4, :]
    scale = mod_ref[0, 4:5, :]
    nv = tm // SUBLANES
    pitch = nv + SUBLANES
    rows = tm + 2 * SUBLANES
    n_slab = D_MODEL // LANES

    h_nat = _norm_mod(x_ref[0], gamma, shift, scale)
    for s in range(SUBLANES):
        for k in range(n_slab):
            hnat_ref[k, s * pitch:s * pitch + nv, :] = h_nat[s * nv:(s + 1) * nv, k * LANES:(k + 1) * LANES]

    def gather_rows(ref, start, stride):
        return jnp.concatenate([ref[k, pl.ds(start, SUBLANES, stride=stride), :] for k in range(n_slab)], axis=1)

    for v in range(0, nv, 2):
        pair = jnp.concatenate([gather_rows(hnat_ref, v, pitch), gather_rows(hnat_ref, v + 1, pitch)], axis=0)
        h_ref[v * SUBLANES:(v + 2) * SUBLANES, :] = pair.astype(BF16)
    hp = jnp.where(i > 0, _norm_mod(xp_ref[0], gamma, shift, scale), 0.0)
    hn = jnp.where(i < n_tiles - 1, _norm_mod(xn_ref[0], gamma, shift, scale), 0.0)
    h_ref[tm:rows, :] = jnp.concatenate([hp, hn], axis=0).astype(BF16)

    sub = lax.broadcasted_iota(jnp.int32, (SUBLANES, 1), 0)

    def conv(u, cw):
        main = u[0:tm]
        first_prev = pltpu.roll(jnp.where(sub == SUBLANES - 1, u[tm:tm + SUBLANES], u[tm - SUBLANES:tm]), 1, 0)
        last_next = pltpu.roll(jnp.where(sub == 0, u[tm + SUBLANES:rows], u[0:SUBLANES]), SUBLANES - 1, 0)
        prev = jnp.concatenate([first_prev, u[0:tm - SUBLANES]], axis=0)
        nxt = jnp.concatenate([u[SUBLANES:tm], last_next], axis=0)
        return cw[0:1] * prev + cw[1:2] * main + cw[2:3] * nxt + cw[3:4]

    n_chunks = D_FF // FF_CHUNK
    group_start = 0
    for c in range(n_chunks):
        gs = slice(c * FF_CHUNK, (c + 1) * FF_CHUNK)
        vs = slice(D_FF + c * FF_CHUNK, D_FF + (c + 1) * FF_CHUNK)
        h = h_ref[...]
        ug = jnp.dot(h, wu_ref[:, gs], preferred_element_type=F32)
        uv = jnp.dot(h, wu_ref[:, vs], preferred_element_type=F32)
        a = _silu(conv(ug, cw_ref[:, gs])) * conv(uv, cw_ref[:, vs])
        act_ref[:, gs] = a.astype(BF16)
        if (c + 1) % FF_GROUP == 0 or c == n_chunks - 1:
            ks = slice(group_start * FF_CHUNK, (c + 1) * FF_CHUNK)
            part = jnp.dot(act_ref[:, ks], wd_ref[ks, :], preferred_element_type=F32)
            for k in range(n_slab):
                if group_start == 0:
                    acc_ref[k] = part[:, k * LANES:(k + 1) * LANES]
                else:
                    acc_ref[k] += part[:, k * LANES:(k + 1) * LANES]
            group_start = c + 1
    gate = mod_ref[0, 5:6, :]
    for j in range(nv):
        s, v0 = divmod(j * SUBLANES, nv)
        rs = slice(j * SUBLANES, (j + 1) * SUBLANES)
        y = x_ref[0, rs, :] + gate * gather_rows(acc_ref, v0 * SUBLANES + s, SUBLANES)
        if final_norm:
            ms = jnp.mean(y * y, axis=-1, keepdims=True)
            y = (y * lax.rsqrt(ms + EPS)) * fg_ref[...]
        out_ref[0, rs, :] = y


def _ffn(x, gamma, mod, w_up, cw, w_down, final_gamma, *, tm, final_norm, name):
    b, l, d = x.shape
    n_tiles = l // tm
    bpt = tm // SUBLANES
    nblk8 = l // SUBLANES
    const2 = lambda bi, i: (0, 0)
    rows = tm + 2 * SUBLANES
    kern = functools.partial(_ffn_kernel, tm=tm, n_tiles=n_tiles, final_norm=final_norm)
    return pl.pallas_call(
        kern,
        grid=(b, n_tiles),
        in_specs=[
            pl.BlockSpec((1, tm, d), lambda bi, i: (bi, i, 0)),
            pl.BlockSpec((1, SUBLANES, d), lambda bi, i: (bi, jnp.maximum(i * bpt - 1, 0), 0)),
            pl.BlockSpec((1, SUBLANES, d), lambda bi, i: (bi, jnp.minimum((i + 1) * bpt, nblk8 - 1), 0)),
            pl.BlockSpec((1, d), const2),
            _mod_spec(mod),
            pl.BlockSpec(w_up.shape, const2),
            pl.BlockSpec(cw.shape, const2),
            pl.BlockSpec(w_down.shape, const2),
            pl.BlockSpec((1, d), const2),
        ],
        out_specs=pl.BlockSpec((1, tm, d), lambda bi, i: (bi, i, 0)),
        out_shape=jax.ShapeDtypeStruct((b, l, d), F32),
        scratch_shapes=[pltpu.VMEM((rows, d), BF16),
                        pltpu.VMEM((d // LANES, SUBLANES * (tm // SUBLANES + SUBLANES), LANES), F32),
                        pltpu.VMEM((d // LANES, tm, LANES), F32),
                        pltpu.VMEM((tm, D_FF), BF16)],
        compiler_params=_cparams(("parallel", "parallel")),
        name=name,
    )(x, x, x, gamma.reshape(1, d), mod, w_up, cw, w_down, final_gamma.reshape(1, d))


def _gqa_core(sink_ref, q_ref, kcat, vcat, bias, o_ref, tq):
    lane_head = lax.broadcasted_iota(jnp.int32, (1, WA_KV_HEADS * WA_HEAD_DIM), 1) // WA_HEAD_DIM
    width = WA_KV_HEADS * WA_HEAD_DIM
    for g in range(WA_GROUP):
        qg = q_ref[0, :, g * width:(g + 1) * width]
        q4 = jnp.concatenate([jnp.where(lane_head == h, qg, jnp.zeros_like(qg)) for h in range(WA_KV_HEADS)], axis=0)
        s = _dot_nt(q4, kcat)
        if bias is not None:
            s = s + jnp.concatenate([bias] * WA_KV_HEADS, axis=0)
        sk = jnp.concatenate([jnp.full((tq, 1), sink_ref[h * WA_GROUP + g], F32) for h in range(WA_KV_HEADS)], axis=0)
        m = jnp.maximum(jnp.max(s, axis=-1, keepdims=True), sk)
        p = jnp.exp(s - m)
        denom = jnp.sum(p, axis=-1, keepdims=True) + jnp.exp(sk - m)
        o4 = jnp.dot(p.astype(BF16), vcat, preferred_element_type=F32) / denom
        og = jnp.zeros((tq, width), F32)
        for h in range(WA_KV_HEADS):
            og = og + jnp.where(lane_head == h, o4[h * tq:(h + 1) * tq], 0.0)
        o_ref[0, :, g * width:(g + 1) * width] = og.astype(o_ref.dtype)


def _win_ctx_kernel(sink_ref, q_ref, k_ref, v_ref, o_ref, *, tq):
    _gqa_core(sink_ref, q_ref, k_ref[0].astype(BF16), v_ref[0].astype(BF16), None, o_ref, tq)


def _win_ctx_attn(q, k, v, sink, *, tq):
    b, l, _ = q.shape
    kvw = WA_KV_HEADS * WA_HEAD_DIM
    return pl.pallas_call(
        functools.partial(_win_ctx_kernel, tq=tq),
        grid=(b, l // tq),
        in_specs=[
            pl.BlockSpec(memory_space=pltpu.SMEM),
            pl.BlockSpec((1, tq, D_MODEL), lambda bi, i: (bi, i, 0)),
            pl.BlockSpec((1, l, kvw), lambda bi, i: (bi, 0, 0)),
            pl.BlockSpec((1, l, kvw), lambda bi, i: (bi, 0, 0)),
        ],
        out_specs=pl.BlockSpec((1, tq, D_MODEL), lambda bi, i: (bi, i, 0)),
        out_shape=jax.ShapeDtypeStruct((b, l, D_MODEL), BF16),
        compiler_params=_cparams(("parallel", "parallel")),
        name="win_ctx_attn",
    )(sink, q, k, v)


def _win_lat_kernel(sink_ref, q_ref, kp_ref, kc_ref, kn_ref, vp_ref, vc_ref, vn_ref, kx_ref, vx_ref, o_ref,
                    kcat, vcat, *, tq, seq_len, n_ctx):
    i = pl.program_id(1)
    kcat[0:tq] = kp_ref[0]
    kcat[tq:2 * tq] = kc_ref[0]
    kcat[2 * tq:3 * tq] = kn_ref[0]
    kcat[3 * tq:3 * tq + n_ctx] = kx_ref[0]
    vcat[0:tq] = vp_ref[0]
    vcat[tq:2 * tq] = vc_ref[0]
    vcat[2 * tq:3 * tq] = vn_ref[0]
    vcat[3 * tq:3 * tq + n_ctx] = vx_ref[0]
    nk = 3 * tq + n_ctx
    r = lax.broadcasted_iota(jnp.int32, (tq, nk), 0)
    c = lax.broadcasted_iota(jnp.int32, (tq, nk), 1)
    qpos = i * tq + r
    kpos = (i - 1) * tq + c
    ok = (c >= 3 * tq) | ((kpos >= 0) & (kpos < seq_len) & (jnp.abs(qpos - kpos) <= WINDOW))
    bias = jnp.where(ok, 0.0, NEG_BIG).astype(F32)
    _gqa_core(sink_ref, q_ref, kcat[...], vcat[...], bias, o_ref, tq)


def _win_lat_attn(q, k, v, kctx, vctx, sink):
    b, l, _ = q.shape
    tq = WINDOW
    nq = l // tq
    n_ctx = kctx.shape[1]
    kvw = WA_KV_HEADS * WA_HEAD_DIM
    prev = lambda bi, i: (bi, jnp.maximum(i - 1, 0), 0)
    cur = lambda bi, i: (bi, i, 0)
    nxt = lambda bi, i: (bi, jnp.minimum(i + 1, nq - 1), 0)
    kv_spec = lambda f: pl.BlockSpec((1, tq, kvw), f)
    return pl.pallas_call(
        functools.partial(_win_lat_kernel, tq=tq, seq_len=l, n_ctx=n_ctx),
        grid=(b, nq),
        in_specs=[
            pl.BlockSpec(memory_space=pltpu.SMEM),
            pl.BlockSpec((1, tq, D_MODEL), cur),
            kv_spec(prev), kv_spec(cur), kv_spec(nxt),
            kv_spec(prev), kv_spec(cur), kv_spec(nxt),
            pl.BlockSpec((1, n_ctx, kvw), lambda bi, i: (bi, 0, 0)),
            pl.BlockSpec((1, n_ctx, kvw), lambda bi, i: (bi, 0, 0)),
        ],
        out_specs=pl.BlockSpec((1, tq, D_MODEL), cur),
        out_shape=jax.ShapeDtypeStruct((b, l, D_MODEL), BF16),
        scratch_shapes=[pltpu.VMEM((3 * tq + n_ctx, kvw), BF16), pltpu.VMEM((3 * tq + n_ctx, kvw), BF16)],
        compiler_params=_cparams(("parallel", "parallel")),
        name="win_lat_attn",
    )(sink, q, k, k, k, v, v, v, kctx, vctx)


def _diff_kernel(lqk_ref, gsub_ref, q_ref, k_ref, v_ref, *rest, tq, kc, n_keys, has_ctx, lam_init):
    if has_ctx:
        kx_ref, vx_ref, o_ref, kall, vall, m_ref, acc_ref, s_a, s_b = rest
    else:
        o_ref, kall, vall, m_ref, acc_ref, s_a, s_b = rest
    hd = DA_HEAD_DIM
    lam = (jnp.exp(jnp.sum(lqk_ref[0:1, :] * lqk_ref[1:2, :], axis=-1, keepdims=True))
           - jnp.exp(jnp.sum(lqk_ref[2:3, :] * lqk_ref[3:4, :], axis=-1, keepdims=True)) + lam_init)
    hw = 2 * hd

    @pl.when(pl.program_id(2) == 0)
    def _():
        lat = k_ref.shape[1]
        kall[0:lat, :] = k_ref[0].astype(BF16)
        vall[0:lat, 0:hw] = v_ref[0].astype(BF16)
        if has_ctx:
            kall[lat:n_keys, :] = kx_ref[0]
            vall[lat:n_keys, 0:hw] = vx_ref[0]
        vall[:, hw:2 * hw] = jnp.ones((n_keys, hw), BF16)

    q = q_ref[0]
    lane = lax.broadcasted_iota(jnp.int32, (1, hw), 1)
    zero = jnp.zeros_like(q)
    q2 = jnp.concatenate([jnp.where(lane < hd, q, zero), jnp.where(lane >= hd, q, zero)], axis=0)

    n_chunks = n_keys // kc
    s_bufs = (s_a, s_b)
    s_bufs[0][...] = _dot_nt(q2, kall[0:kc, :])
    for j in range(n_chunks):
        if j + 1 < n_chunks:
            s_bufs[(j + 1) % 2][...] = _dot_nt(q2, kall[(j + 1) * kc:(j + 2) * kc, :])
        s = s_bufs[j % 2][...]
        m_chunk = jnp.max(s, axis=-1, keepdims=True)
        if j == 0:
            m_new = jnp.broadcast_to(m_chunk, m_ref.shape)
        else:
            m_old = m_ref[...]
            m_new = jnp.maximum(m_old, m_chunk)
            alpha = jnp.exp(m_old - m_new)
        p = jnp.exp(s - jnp.concatenate([m_new] * (kc // LANES), axis=1))
        pv = jnp.dot(p.astype(BF16), vall[j * kc:(j + 1) * kc, :], preferred_element_type=F32)
        if j == 0:
            acc_ref[...] = pv
        else:
            acc_ref[...] = jnp.concatenate([alpha, alpha], axis=1) * acc_ref[...] + pv
        if j + 1 < n_chunks:
            m_ref[...] = m_new
    acc = acc_ref[...]
    o2 = acc[:, 0:hw] / acc[:, hw:2 * hw]
    o = o2[0:tq] - lam * o2[tq:2 * tq]
    ms = jnp.mean(o * o, axis=-1, keepdims=True)
    o = (o * lax.rsqrt(ms + EPS)) * gsub_ref[...] * (1.0 - lam_init)
    o_ref[0] = o.astype(o_ref.dtype)


def _diff_attn(q, k, v, kctx, vctx, lqk, gsub, *, tq, kc, lam_init):
    b, l, _ = q.shape
    hw = 2 * DA_HEAD_DIM
    has_ctx = kctx is not None
    in_specs = [
        pl.BlockSpec((8, hw), lambda bi, h, i: (0, 0)),
        pl.BlockSpec((1, hw), lambda bi, h, i: (0, 0)),
        pl.BlockSpec((1, tq, hw), lambda bi, h, i: (bi, i, h)),
        pl.BlockSpec((1, l, hw), lambda bi, h, i: (bi, 0, h)),
        pl.BlockSpec((1, l, hw), lambda bi, h, i: (bi, 0, h)),
    ]
    args = [lqk, gsub, q, k, v]
    n_keys = l
    if has_ctx:
        n_ctx = kctx.shape[1]
        n_keys = l + n_ctx
        in_specs.append(pl.BlockSpec((1, n_ctx, hw), lambda bi, h, i: (bi, 0, h)))
        in_specs.append(pl.BlockSpec((1, n_ctx, hw), lambda bi, h, i: (bi, 0, h)))
        args += [kctx, vctx]
    assert n_keys % kc == 0 and kc % LANES == 0
    kern = functools.partial(_diff_kernel, tq=tq, kc=kc, n_keys=n_keys, has_ctx=has_ctx, lam_init=lam_init)
    return pl.pallas_call(
        kern,
        grid=(b, DA_HEADS, l // tq),
        in_specs=in_specs,
        out_specs=pl.BlockSpec((1, tq, hw), lambda bi, h, i: (bi, i, h)),
        out_shape=jax.ShapeDtypeStruct((b, l, D_MODEL), BF16),
        scratch_shapes=[pltpu.VMEM((n_keys, hw), BF16), pltpu.VMEM((n_keys, 2 * hw), BF16),
                        pltpu.VMEM((2 * tq, LANES), F32), pltpu.VMEM((2 * tq, 2 * hw), F32),
                        pltpu.VMEM((2 * tq, kc), F32), pltpu.VMEM((2 * tq, kc), F32)],
        compiler_params=_cparams(("parallel", "parallel", "arbitrary")),
        name="diff_lat_attn" if has_ctx else "diff_ctx_attn",
    )(*args)


def _gla_scan_kernel(q_ref, k_ref, v_ref, zl_ref, w2_ref, b2_ref, s0_ref, o_ref, sf_ref,
                     st_scr, qd_scr, kd_scr, kr_scr, vt_scr, mt_scr, el_scr, *, blk, reverse, n_blocks):
    i = pl.program_id(1)
    ch = GLA_CHUNK
    nk = GLA_HEADS * GLA_DK
    n_sub = blk // ch

    @pl.when(i == 0)
    def _():
        for h in range(GLA_HEADS):
            st_scr[h] = jnp.transpose(s0_ref[0, h])

    cs = min(blk, GLA_CUM_ROWS)
    r = lax.broadcasted_iota(jnp.int32, (cs, cs), 0)
    c = lax.broadcasted_iota(jnp.int32, (cs, cs), 1)
    same_chunk = (r // ch) == (c // ch)
    tri = (same_chunk & ((r <= c) if reverse else (r >= c))).astype(BF16)
    rr = lax.broadcasted_iota(jnp.int32, (ch, ch), 0)
    cc = lax.broadcasted_iota(jnp.int32, (ch, ch), 1)
    keep = (rr <= cc) if reverse else (rr >= cc)
    edge = 0 if reverse else ch - 1

    z = jnp.dot(zl_ref[0].astype(BF16), w2_ref[...], preferred_element_type=F32) + b2_ref[...]
    logg = (jnp.minimum(z, 0.0) - jnp.log1p(jnp.exp(-jnp.abs(z)))) / GLA_TAU
    cum = jnp.concatenate([_exact_dot_left01(tri, logg[t:t + cs]) for t in range(0, blk, cs)], axis=0)
    lasts = [cum[s * ch + edge:s * ch + edge + 1, :] for s in range(n_sub)]
    blast = jnp.concatenate([jnp.broadcast_to(b, (ch, nk)) for b in lasts], axis=0)
    k = k_ref[0]
    qd_scr[...] = ((q_ref[0] * (GLA_DK ** -0.5)) * jnp.exp(cum)).astype(BF16)
    kd_scr[...] = (k * jnp.exp(-cum)).astype(BF16)
    kr_scr[...] = (k * jnp.exp(blast - cum)).astype(BF16)
    for s in range(n_sub):
        el_scr[s:s + 1, :] = jnp.exp(lasts[s])
    vt_scr[...] = jnp.transpose(v_ref[0])

    lane_chunk = lax.broadcasted_iota(jnp.int32, (1, 2 * ch), 1) // ch
    for s in range(n_sub):
        rows = slice(s * ch, (s + 1) * ch)
        pair = slice((s // 2) * 2 * ch, (s // 2 + 1) * 2 * ch)
        for h in range(GLA_HEADS):
            ks = slice(h * GLA_DK, (h + 1) * GLA_DK)
            vs = slice(h * GLA_DV, (h + 1) * GLA_DV)
            att = jnp.where(keep, _dot_nt(qd_scr[rows, ks], kd_scr[rows, ks]), 0.0)
            o_ref[0, rows, vs] = jnp.dot(att.astype(BF16), v_ref[0, rows, vs].astype(BF16),
                                         preferred_element_type=F32)
            vt = jnp.where(lane_chunk == s % 2, vt_scr[vs, pair], 0.0).astype(BF16)
            mt_scr[s, h] = jnp.dot(vt, kr_scr[pair, ks], preferred_element_type=F32)

    order = range(n_sub - 1, -1, -1) if reverse else range(n_sub)
    for s in order:
        rows = slice(s * ch, (s + 1) * ch)
        for h in range(GLA_HEADS):
            ks = slice(h * GLA_DK, (h + 1) * GLA_DK)
            vs = slice(h * GLA_DV, (h + 1) * GLA_DV)
            st = st_scr[h]
            o_ref[0, rows, vs] += _dot_nt(qd_scr[rows, ks], st.astype(BF16))
            st_scr[h] = st * el_scr[s:s + 1, ks] + mt_scr[s, h]

    @pl.when(i == n_blocks - 1)
    def _():
        for h in range(GLA_HEADS):
            sf_ref[0, h] = jnp.transpose(st_scr[h])


def _gla_scan(q, k, v, zl, w2, b2, s0, *, blk, reverse):
    b, l, nk = q.shape
    nv = v.shape[-1]
    n_blocks = l // blk
    assert blk % (2 * GLA_CHUNK) == 0 and blk // GLA_CHUNK <= SUBLANES
    blk_idx = (lambda bi, i: (bi, n_blocks - 1 - i, 0)) if reverse else (lambda bi, i: (bi, i, 0))
    state_spec = pl.BlockSpec((1, GLA_HEADS, GLA_DK, GLA_DV), lambda bi, i: (bi, 0, 0, 0))
    return pl.pallas_call(
        functools.partial(_gla_scan_kernel, blk=blk, reverse=reverse, n_blocks=n_blocks),
        grid=(b, n_blocks),
        in_specs=[
            pl.BlockSpec((1, blk, nk), blk_idx),
            pl.BlockSpec((1, blk, nk), blk_idx),
            pl.BlockSpec((1, blk, nv), blk_idx),
            pl.BlockSpec((1, blk, LANES), blk_idx),
            pl.BlockSpec((LANES, nk), lambda bi, i: (0, 0)),
            pl.BlockSpec((1, nk), lambda bi, i: (0, 0)),
            state_spec,
        ],
        out_specs=[pl.BlockSpec((1, blk, nv), blk_idx), state_spec],
        out_shape=[jax.ShapeDtypeStruct((b, l, nv), F32),
                   jax.ShapeDtypeStruct((b, GLA_HEADS, GLA_DK, GLA_DV), F32)],
        scratch_shapes=[pltpu.VMEM((GLA_HEADS, GLA_DV, GLA_DK), F32),
                        pltpu.VMEM((blk, nk), BF16), pltpu.VMEM((blk, nk), BF16), pltpu.VMEM((blk, nk), BF16),
                        pltpu.VMEM((nv, blk), F32),
                        pltpu.VMEM((blk // GLA_CHUNK, GLA_HEADS, GLA_DV, GLA_DK), F32),
                        pltpu.VMEM((SUBLANES, nk), F32)],
        compiler_params=_cparams(("parallel", "arbitrary")),
        name="gla_scan_bwd" if reverse else "gla_scan_fwd",
    )(q, k, v, zl, w2, b2, s0)


def _gla_out_kernel(of_ref, ob_ref, r_ref, gh_ref, w_ref, x_ref, mod_ref, out_ref):
    o = of_ref[0] + ob_ref[0]
    parts = []
    for h in range(GLA_HEADS):
        oh = o[:, h * GLA_DV:(h + 1) * GLA_DV]
        ms = jnp.mean(oh * oh, axis=-1, keepdims=True)
        parts.append((oh * lax.rsqrt(ms + EPS)) * gh_ref[...])
    y = jnp.concatenate(parts, axis=1) * _silu(r_ref[0])
    acc = jnp.dot(y.astype(BF16), w_ref[...], preferred_element_type=F32)
    out_ref[0] = x_ref[0] + mod_ref[0, 2:3, :] * acc


def _gla_out(o_f, o_b, r, g_head, w, x, mod, *, tm):
    b, l, nv = o_f.shape
    row = lambda bi, i: (bi, i, 0)
    return pl.pallas_call(
        _gla_out_kernel,
        grid=(b, l // tm),
        in_specs=[
            pl.BlockSpec((1, tm, nv), row),
            pl.BlockSpec((1, tm, nv), row),
            pl.BlockSpec((1, tm, nv), row),
            pl.BlockSpec((1, GLA_DV), lambda bi, i: (0, 0)),
            pl.BlockSpec((nv, D_MODEL), lambda bi, i: (0, 0)),
            pl.BlockSpec((1, tm, D_MODEL), row),
            _mod_spec(mod),
        ],
        out_specs=pl.BlockSpec((1, tm, D_MODEL), row),
        out_shape=jax.ShapeDtypeStruct((b, l, D_MODEL), F32),
        compiler_params=_cparams(("parallel", "parallel")),
        name="gla_out",
    )(o_f, o_b, r, g_head.reshape(1, GLA_DV), w, x, mod)


def _ssd_scan_kernel(x_ref, b_ref, c_ref, dt_ref, bias_ref, a_ref, dsk_ref, s0_ref, y_ref, sf_ref, s_scr, xt_scr,
                     *, reverse, n_chunks, lane_off, add_skip):
    i = pl.program_id(1)
    ch = SSD_CHUNK
    p = SSD_HEAD_DIM
    hpg = SSD_HEADS // SSD_GROUPS

    @pl.when(i == 0)
    def _():
        s_scr[...] = s0_ref[0]

    r = lax.broadcasted_iota(jnp.int32, (ch, ch), 0)
    c = lax.broadcasted_iota(jnp.int32, (ch, ch), 1)
    keep = (r <= c) if reverse else (r >= c)
    tri = keep.astype(BF16)
    edge = 0 if reverse else ch - 1

    dt = _softplus(dt_ref[0] + bias_ref[...])
    la = dt * a_ref[...]
    cum = _exact_dot_left01(tri, la) * LOG2_E
    cum_t = jnp.transpose(cum)
    dt_t = jnp.transpose(dt)
    to_end_t = jnp.transpose(jnp.exp2(cum[edge:edge + 1, :] - cum) * dt)
    xt_scr[...] = jnp.transpose(x_ref[0])
    keep_f = keep.astype(F32)
    lane = lax.broadcasted_iota(jnp.int32, (1, LANES), 1)
    lo_half = lane < p

    for g in range(SSD_GROUPS):
        bg = b_ref[0, :, g * SSD_STATE:(g + 1) * SSD_STATE].astype(BF16)
        cg = c_ref[0, :, g * SSD_STATE:(g + 1) * SSD_STATE].astype(BF16)
        cb = _dot_nt(cg, bg) * keep_f
        gs = slice(g * hpg * p, (g + 1) * hpg * p)
        s_g = s_scr[gs, :]
        y_inter = _dot_nt(cg, s_g.astype(BF16))
        for pair in range(hpg // 2):
            cols = slice((g * hpg + 2 * pair) * p, (g * hpg + 2 * pair + 2) * p)
            x_pair = x_ref[0, :, cols]
            ws = []
            e_is = []
            for t in range(2):
                hl = lane_off + g * hpg + 2 * pair + t
                cum_i = jnp.broadcast_to(cum[:, hl:hl + 1], (ch, ch))
                seg = jnp.minimum(cum_i - cum_t[hl:hl + 1, :], 0.0)
                ws.append((cb * jnp.exp2(seg) * dt_t[hl:hl + 1, :]).astype(BF16))
                e_is.append(jnp.exp2(cum_i))
            xb = x_pair.astype(BF16)
            zero = jnp.zeros_like(xb)
            x_bd = jnp.concatenate([jnp.where(lo_half, xb, zero), jnp.where(lo_half, zero, xb)], axis=0)
            y_pair = jnp.dot(jnp.concatenate(ws, axis=1), x_bd, preferred_element_type=F32)
            y_pair = y_pair + y_inter[:, 2 * pair * p:(2 * pair + 2) * p] * jnp.where(lo_half, e_is[0], e_is[1])
            if add_skip:
                hl0 = g * hpg + 2 * pair
                dsk = jnp.where(lo_half, dsk_ref[:, hl0:hl0 + 1], dsk_ref[:, hl0 + 1:hl0 + 2])
                y_pair = y_pair + x_pair * dsk
            y_ref[0, :, cols] = y_pair
        xs_t = []
        for hh in range(hpg):
            hl = lane_off + g * hpg + hh
            rs = slice((g * hpg + hh) * p, (g * hpg + hh + 1) * p)
            xs_t.append((xt_scr[rs, :] * to_end_t[hl:hl + 1, :]).astype(BF16))
        ds = jnp.dot(jnp.concatenate(xs_t, axis=0), bg, preferred_element_type=F32)
        for hh in range(hpg):
            hl = lane_off + g * hpg + hh
            tot = jnp.exp2(cum_t[hl:hl + 1, edge:edge + 1])
            rs = slice((g * hpg + hh) * p, (g * hpg + hh + 1) * p)
            s_scr[rs, :] = s_scr[rs, :] * tot + ds[hh * p:(hh + 1) * p, :]

    @pl.when(i == n_chunks - 1)
    def _():
        sf_ref[0] = s_scr[...]


def _ssd_scan(xbc, dt, bias, a, dskip, s0, *, reverse, add_skip):
    b, l, _ = xbc.shape
    ch = SSD_CHUNK
    n_chunks = l // ch
    gn = SSD_GROUPS * SSD_STATE
    rows = SSD_HEADS * SSD_HEAD_DIM

    def at(col):
        if reverse:
            return lambda bi, i: (bi, n_chunks - 1 - i, col)
        return lambda bi, i: (bi, i, col)

    vec = pl.BlockSpec((1, LANES), lambda bi, i: (0, 0))
    state_spec = pl.BlockSpec((1, rows, SSD_STATE), lambda bi, i: (bi, 0, 0))
    kern = functools.partial(_ssd_scan_kernel, reverse=reverse, n_chunks=n_chunks,
                             lane_off=SSD_HEADS if reverse else 0, add_skip=add_skip)
    return pl.pallas_call(
        kern,
        grid=(b, n_chunks),
        in_specs=[
            pl.BlockSpec((1, ch, SSD_D_INNER), at(0)),
            pl.BlockSpec((1, ch, gn), at(SSD_D_INNER // gn)),
            pl.BlockSpec((1, ch, gn), at(SSD_D_INNER // gn + 1)),
            pl.BlockSpec((1, ch, LANES), at(0)),
            vec, vec, vec,
            state_spec,
        ],
        out_specs=[pl.BlockSpec((1, ch, SSD_D_INNER), at(0)), state_spec],
        out_shape=[jax.ShapeDtypeStruct((b, l, SSD_D_INNER), F32),
                   jax.ShapeDtypeStruct((b, rows, SSD_STATE), F32)],
        scratch_shapes=[pltpu.VMEM((rows, SSD_STATE), F32), pltpu.VMEM((SSD_D_INNER, ch), F32)],
        compiler_params=_cparams(("parallel", "arbitrary")),
        name="ssd_scan_bwd" if reverse else "ssd_scan_fwd",
    )(xbc, xbc, xbc, dt, bias, a, dskip, s0)


def _ssd_out_kernel(yf_ref, yb_ref, z_ref, gn_ref, w_ref, x_ref, mod_ref, out_ref):
    y = (yf_ref[0] + yb_ref[0]) * _silu(z_ref[0])
    ms = jnp.mean(y * y, axis=-1, keepdims=True)
    y = (y * lax.rsqrt(ms + EPS)) * gn_ref[...]
    acc = jnp.dot(y.astype(BF16), w_ref[...], preferred_element_type=F32)
    out_ref[0] = x_ref[0] + mod_ref[0, 2:3, :] * acc


def _ssd_out(y_f, y_b, z, g_norm, w, x, mod, *, tm):
    b, l, di = y_f.shape
    row = lambda bi, i: (bi, i, 0)
    return pl.pallas_call(
        _ssd_out_kernel,
        grid=(b, l // tm),
        in_specs=[
            pl.BlockSpec((1, tm, di), row),
            pl.BlockSpec((1, tm, di), row),
            pl.BlockSpec((1, tm, di), row),
            pl.BlockSpec((1, di), lambda bi, i: (0, 0)),
            pl.BlockSpec((di, D_MODEL), lambda bi, i: (0, 0)),
            pl.BlockSpec((1, tm, D_MODEL), row),
            _mod_spec(mod),
        ],
        out_specs=pl.BlockSpec((1, tm, D_MODEL), row),
        out_shape=jax.ShapeDtypeStruct((b, l, D_MODEL), F32),
        compiler_params=_cparams(("parallel", "parallel")),
        name="ssd_out",
    )(y_f, y_b, z, g_norm.reshape(1, di), w, x, mod)


def _rope_tables(n_tokens, dim):
    rows = n_tokens // GRID_W
    row = jnp.repeat(jnp.arange(rows, dtype=F32), GRID_W)
    col = jnp.tile(jnp.arange(GRID_W, dtype=F32), rows)
    axis_dim = dim // 2
    inv = ROPE_BASE ** (-jnp.arange(0, axis_dim, 2, dtype=F32) / axis_dim)
    ar = row[:, None] * inv
    ac = col[:, None] * inv
    cos = jnp.concatenate([jnp.cos(ar), jnp.cos(ar), jnp.cos(ac), jnp.cos(ac)], axis=1)
    sin = jnp.concatenate([-jnp.sin(ar), jnp.sin(ar), -jnp.sin(ac), jnp.sin(ac)], axis=1)
    reps = LANES // dim
    return jnp.tile(cos, (1, reps)), jnp.tile(sin, (1, reps))


def _pad_cols(w, width):
    return jnp.pad(w, ((0, 0), (0, width - w.shape[1])))


def _conv_pack(conv_w, conv_b):
    return jnp.concatenate([conv_w, conv_b[None], jnp.zeros((4, conv_w.shape[1]), F32)], axis=0)


def _tm_for(l):
    return min(l, 512)


def _key_chunk(n_keys):
    for kc in (768, 512, 256):
        if n_keys % kc == 0:
            return kc
    return n_keys


def kernel(x_prompt, x_sample, c, c_ctx, cache_win_k, cache_win_v, state_gla_fwd, state_gla_bwd, cache_diff_k, cache_diff_v, state_ssd_fwd, state_ssd_bwd, ada_w, ada_b, norm_mix, norm_ffn, ffn_w_up, ffn_conv_w, ffn_conv_b, ffn_w_down, final_norm, win_w_qkv, win_w_o, win_sink, gla_w_qkvr, gla_w_gf1, gla_w_gf2, gla_b_gf, gla_w_gb1, gla_w_gb2, gla_b_gb, gla_norm, gla_w_o, diff_w_qkv, diff_lq1, diff_lk1, diff_lq2, diff_lk2, diff_norm, diff_w_o, ssd_w_in, ssd_conv_w, ssd_conv_b, ssd_a_log_f, ssd_a_log_b, ssd_dt_bias_f, ssd_dt_bias_b, ssd_d, ssd_norm, ssd_w_out):
    xp, xs = x_prompt, x_sample
    bp, lp, d = xp.shape
    bs, ls, _ = xs.shape
    tmp, tms = _tm_for(lp), _tm_for(ls)

    n_cond = 1 + bs
    cond_rows = -(-n_cond // SUBLANES) * SUBLANES
    cond = jnp.concatenate([c_ctx[None], c, jnp.zeros((cond_rows - n_cond, d), F32)], axis=0)
    mods = _ada_call(cond, ada_w, ada_b).reshape(DEPTH, cond_rows, 6, d)
    mods = jnp.pad(mods, ((0, 0), (0, 0), (0, 2), (0, 0)))

    outs = {}
    for i in range(DEPTH):
        kind, j = i % 4, i // 4
        mod_p = mods[i, 0:1]
        mod_s = mods[i, 1:1 + bs]
        if kind == 0:
            nq = WA_HEADS * WA_HEAD_DIM
            nkv = WA_KV_HEADS * WA_HEAD_DIM
            perm = jnp.arange(nq).reshape(WA_KV_HEADS, WA_GROUP, WA_HEAD_DIM).transpose(1, 0, 2).reshape(-1)
            wq = win_w_qkv[j][:, :nq][:, perm]
            w = jnp.concatenate([wq, win_w_qkv[j][:, nq:]], axis=1).astype(BF16)
            w_o = win_w_o[j][perm, :].astype(BF16)
            sink = win_sink[j]
            qscale = WA_HEAD_DIM ** -0.5
            q, k, v = _fused_proj(xp, norm_mix[i], mod_p, w,
                                  [(nq, "plain", qscale), (nkv, "plain", 1.0), (nkv, "plain", 1.0)],
                                  [BF16, F32, F32], tm=tmp, name="win_proj_ctx")
            outs["win_k"] = k.reshape(bp, 1, lp, WA_KV_HEADS, WA_HEAD_DIM)
            outs["win_v"] = v.reshape(bp, 1, lp, WA_KV_HEADS, WA_HEAD_DIM)
            o = _win_ctx_attn(q, k, v, sink, tq=min(lp, 128))
            xp = _out_proj(o, w_o, xp, mod_p, tm=tmp, name="win_out_ctx")
            rope = _rope_tables(ls, WA_HEAD_DIM)
            q, k, v = _fused_proj(xs, norm_mix[i], mod_s, w,
                                  [(nq, "rope", qscale), (nkv, "rope", 1.0), (nkv, "plain", 1.0)],
                                  [BF16, BF16, BF16], tm=tms, rope=rope, name="win_proj_lat")
            n_ctx = cache_win_k.shape[2]
            kctx = cache_win_k[:, j].reshape(bs, n_ctx, nkv).astype(BF16)
            vctx = cache_win_v[:, j].reshape(bs, n_ctx, nkv).astype(BF16)
            o = _win_lat_attn(q, k, v, kctx, vctx, sink)
            xs = _out_proj(o, w_o, xs, mod_s, tm=tms, name="win_out_lat")
        elif kind == 1:
            nk = GLA_HEADS * GLA_DK
            nv = GLA_HEADS * GLA_DV
            w1 = _pad_cols(jnp.concatenate([gla_w_gf1[j], gla_w_gb1[j]], axis=1), LANES)
            w = jnp.concatenate([gla_w_qkvr[j], w1], axis=1).astype(BF16)
            segs = [(nk, "plain", 1.0), (nk, "plain", 1.0), (nv, "plain", 1.0), (nv, "plain", 1.0),
                    (LANES, "plain", 1.0)]
            zrows = jnp.zeros((LANES - 2 * GLA_RANK, nk), F32)
            w2_f = jnp.concatenate([gla_w_gf2[j], jnp.zeros((GLA_RANK, nk), F32), zrows], axis=0).astype(BF16)
            w2_b = jnp.concatenate([jnp.zeros((GLA_RANK, nk), F32), gla_w_gb2[j], zrows], axis=0).astype(BF16)
            b2_f = gla_b_gf[j].reshape(1, nk)
            b2_b = gla_b_gb[j].reshape(1, nk)
            w_o = gla_w_o[j].astype(BF16)
            for stream in ("p", "s"):
                if stream == "p":
                    x, mod, tm, bsz = xp, mod_p, tmp, bp
                    s0_f = jnp.zeros((bp, GLA_HEADS, GLA_DK, GLA_DV), F32)
                    s0_b = s0_f
                else:
                    x, mod, tm, bsz = xs, mod_s, tms, bs
                    s0_f = state_gla_fwd[:, j]
                    s0_b = state_gla_bwd[:, j]
                q, k, v, r, zl = _fused_proj(x, norm_mix[i], mod, w, segs, [F32] * 5, tm=tm,
                                             name="gla_proj_" + stream)
                blk = min(x.shape[1], 512)
                o_f, s_f = _gla_scan(q, k, v, zl, w2_f, b2_f, s0_f, blk=blk, reverse=False)
                o_b, s_b = _gla_scan(q, k, v, zl, w2_b, b2_b, s0_b, blk=blk, reverse=True)
                x = _gla_out(o_f, o_b, r, gla_norm[j], w_o, x, mod, tm=tm)
                if stream == "p":
                    xp = x
                    outs["gla_f"] = s_f[:, None]
                    outs["gla_b"] = s_b[:, None]
                else:
                    xs = x
        elif kind == 2:
            lam_init = 0.8 - 0.6 * math.exp(-0.3 * i)
            nh = DA_HEADS * 2 * DA_HEAD_DIM
            w = diff_w_qkv[j].astype(BF16)
            w_o = diff_w_o[j].astype(BF16)
            qscale = DA_HEAD_DIM ** -0.5
            lqk = jnp.stack([diff_lq1[j], diff_lk1[j], diff_lq2[j], diff_lk2[j]], axis=0)
            lqk = jnp.pad(lqk, ((0, 4), (0, 2 * DA_HEAD_DIM - lqk.shape[1])))
            gsub = diff_norm[j].reshape(1, 2 * DA_HEAD_DIM)
            q, k, v = _fused_proj(xp, norm_mix[i], mod_p, w,
                                  [(nh, "plain", qscale), (nh, "plain", 1.0), (nh, "plain", 1.0)],
                                  [BF16, F32, F32], tm=tmp, name="diff_proj_ctx")
            outs["diff_k"] = k.reshape(bp, 1, lp, DA_HEADS, 2, DA_HEAD_DIM)
            outs["diff_v"] = v.reshape(bp, 1, lp, DA_HEADS, 2 * DA_HEAD_DIM)
            o = _diff_attn(q, k, v, None, None, lqk, gsub, tq=min(lp, 256), kc=_key_chunk(lp), lam_init=lam_init)
            xp = _out_proj(o, w_o, xp, mod_p, tm=tmp, name="diff_out_ctx")
            rope = _rope_tables(ls, DA_HEAD_DIM)
            q, k, v = _fused_proj(xs, norm_mix[i], mod_s, w,
                                  [(nh, "rope", qscale), (nh, "rope", 1.0), (nh, "plain", 1.0)],
                                  [BF16, BF16, BF16], tm=tms, rope=rope, name="diff_proj_lat")
            n_ctx = cache_diff_k.shape[2]
            kctx = cache_diff_k[:, j].reshape(bs, n_ctx, nh).astype(BF16)
            vctx = cache_diff_v[:, j].reshape(bs, n_ctx, nh).astype(BF16)
            o = _diff_attn(q, k, v, kctx, vctx, lqk, gsub, tq=min(ls, 512), kc=_key_chunk(ls + n_ctx),
                           lam_init=lam_init)
            xs = _out_proj(o, w_o, xs, mod_s, tm=tms, name="diff_out_lat")
        else:
            gn = SSD_GROUPS * SSD_STATE
            nxbc = SSD_D_INNER + 2 * gn
            w_in = ssd_w_in[j]
            w = jnp.concatenate([w_in[:, :SSD_D_INNER + nxbc], _pad_cols(w_in[:, SSD_D_INNER + nxbc:], LANES)],
                                axis=1).astype(BF16)
            segs = [(SSD_D_INNER, "plain", 1.0), (nxbc, "conv_silu", 1.0), (LANES, "plain", 1.0)]
            cw = _conv_pack(ssd_conv_w[j], ssd_conv_b[j])
            zpad = jnp.zeros((LANES - 2 * SSD_HEADS,), F32)
            zh = jnp.zeros((SSD_HEADS,), F32)
            bias = jnp.concatenate([ssd_dt_bias_f[j], ssd_dt_bias_b[j], zpad]).reshape(1, LANES)
            a_f = jnp.concatenate([-jnp.exp(ssd_a_log_f[j]), zh, zpad]).reshape(1, LANES)
            a_b = jnp.concatenate([zh, -jnp.exp(ssd_a_log_b[j]), zpad]).reshape(1, LANES)
            dsk = jnp.concatenate([ssd_d[j], zh, zpad]).reshape(1, LANES)
            w_out = ssd_w_out[j].astype(BF16)
            rows = SSD_HEADS * SSD_HEAD_DIM
            for stream in ("p", "s"):
                if stream == "p":
                    x, mod, tm, bsz = xp, mod_p, tmp, bp
                    s0_f = jnp.zeros((bp, rows, SSD_STATE), F32)
                    s0_b = s0_f
                else:
                    x, mod, tm, bsz = xs, mod_s, tms, bs
                    s0_f = state_ssd_fwd[:, j].reshape(bs, rows, SSD_STATE)
                    s0_b = state_ssd_bwd[:, j].reshape(bs, rows, SSD_STATE)
                z, xbc, dt = _fused_proj(x, norm_mix[i], mod, w, segs, [F32] * 3, tm=tm, conv_w=cw,
                                         name="ssd_proj_" + stream)
                y_f, s_f = _ssd_scan(xbc, dt, bias, a_f, dsk, s0_f, reverse=False, add_skip=True)
                y_b, s_b = _ssd_scan(xbc, dt, bias, a_b, dsk, s0_b, reverse=True, add_skip=False)
                x = _ssd_out(y_f, y_b, z, ssd_norm[j], w_out, x, mod, tm=tm)
                if stream == "p":
                    xp = x
                    outs["ssd_f"] = s_f.reshape(bp, 1, SSD_HEADS, SSD_HEAD_DIM, SSD_STATE)
                    outs["ssd_b"] = s_b.reshape(bp, 1, SSD_HEADS, SSD_HEAD_DIM, SSD_STATE)
                else:
                    xs = x

        w_up = ffn_w_up[i].astype(BF16)
        cw = _conv_pack(ffn_conv_w[i], ffn_conv_b[i])
        wd = ffn_w_down[i].astype(BF16)
        last = i == DEPTH - 1
        xp = _ffn(xp, norm_ffn[i], mod_p, w_up, cw, wd, final_norm, tm=tmp, final_norm=last, name="ffn_p")
        xs = _ffn(xs, norm_ffn[i], mod_s, w_up, cw, wd, final_norm, tm=tms, final_norm=last, name="ffn_s")

    return (xp, xs, outs["win_k"], outs["win_v"], outs["gla_f"], outs["gla_b"],
            outs["diff_k"], outs["diff_v"], outs["ssd_f"], outs["ssd_b"])
```

```python
import functools
import math

import jax
import jax.numpy as jnp
from jax import lax
from jax.experimental import pallas as pl
from jax.experimental.pallas import tpu as pltpu

F32 = jnp.float32
BF16 = jnp.bfloat16

D_MODEL = 1024
DEPTH = 4
GRID_W = 64
EPS = 1e-6
ROPE_BASE = 10000.0

WA_HEADS = 16
WA_KV_HEADS = 4
WA_GROUP = 4
WA_HEAD_DIM = 64
WINDOW = 128

GLA_HEADS = 4
GLA_DK = 128
GLA_DV = 256
GLA_RANK = 16
GLA_TAU = 16.0
GLA_CHUNK = 64
GLA_CUM_ROWS = 256

DA_HEADS = 8
DA_HEAD_DIM = 64

SSD_D_INNER = 2048
SSD_HEAD_DIM = 64
SSD_HEADS = 32
SSD_GROUPS = 4
SSD_STATE = 128
SSD_CHUNK = 128

D_FF = 2816
FF_CHUNK = 256
FF_GROUP = 4

VMEM_LIMIT_BYTES = 56 * 1024 * 1024
SUBLANES = 8
LANES = 128
NEG_BIG = -1e30
LOG2_E = 1.4426950408889634


def _cparams(sem):
    return pltpu.CompilerParams(dimension_semantics=sem, vmem_limit_bytes=VMEM_LIMIT_BYTES)


def _sigmoid(x):
    return 1.0 / (1.0 + jnp.exp(-x))


def _silu(x):
    return x * _sigmoid(x)


def _softplus(x):
    return jnp.maximum(x, 0.0) + jnp.log1p(jnp.exp(-jnp.abs(x)))


def _norm_mod(x, gamma, shift, scale):
    ms = jnp.mean(x * x, axis=-1, keepdims=True)
    y = (x * lax.rsqrt(ms + EPS)) * gamma
    return y * (1.0 + scale) + shift


def _split3(x):
    hi = x.astype(BF16)
    r1 = x - hi.astype(F32)
    mid = r1.astype(BF16)
    lo = (r1 - mid.astype(F32)).astype(BF16)
    return hi, mid, lo


def _exact_dot_left01(m01, x):
    hi, mid, lo = _split3(x)
    d = lambda p: jnp.dot(m01, p, preferred_element_type=F32)
    return d(hi) + d(mid) + d(lo)


def _dot_nt(a, b):
    return lax.dot_general(a, b, (((1,), (1,)), ((), ())), preferred_element_type=F32)


def _ada_kernel(c_ref, w_ref, b_ref, o_ref):
    h = _silu(c_ref[...]).astype(BF16)
    o_ref[0] = jnp.dot(h, w_ref[0].astype(BF16), preferred_element_type=F32) + b_ref[0]


def _ada_call(cond, ada_w, ada_b):
    rows = cond.shape[0]
    n = ada_w.shape[-1]
    tn = 1536
    return pl.pallas_call(
        _ada_kernel,
        grid=(DEPTH, n // tn),
        in_specs=[
            pl.BlockSpec((rows, D_MODEL), lambda l, j: (0, 0)),
            pl.BlockSpec((1, D_MODEL, tn), lambda l, j: (l, 0, j)),
            pl.BlockSpec((1, 1, tn), lambda l, j: (l, 0, j)),
        ],
        out_specs=pl.BlockSpec((1, rows, tn), lambda l, j: (l, 0, j)),
        out_shape=jax.ShapeDtypeStruct((DEPTH, rows, n), F32),
        compiler_params=_cparams(("parallel", "parallel")),
        name="ada_mod",
    )(cond, ada_w, ada_b.reshape(DEPTH, 1, n))


def _rope_apply(y, cos, sin):
    lane = lax.broadcasted_iota(jnp.int32, (1, LANES), 1)
    first = (lane % 32) < 16
    partner = jnp.where(first, pltpu.roll(y, LANES - 16, 1), pltpu.roll(y, 16, 1))
    return y * cos + partner * sin


def _fused_proj_kernel(*refs, tm, segs, has_conv, has_rope, n_tiles, chunk):
    it = iter(refs)
    x_ref = next(it)
    if has_conv:
        xp_ref = next(it)
        xn_ref = next(it)
    g_ref = next(it)
    mod_ref = next(it)
    w_ref = next(it)
    if has_rope:
        cos_ref = next(it)
        sin_ref = next(it)
    if has_conv:
        cw_ref = next(it)
    out_refs = [next(it) for _ in segs]
    h_ref = next(it)

    i = pl.program_id(1)
    gamma = g_ref[...]
    shift = mod_ref[0, 0:1, :]
    scale = mod_ref[0, 1:2, :]
    off = SUBLANES if has_conv else 0
    h_ref[off:off + tm, :] = _norm_mod(x_ref[0], gamma, shift, scale).astype(BF16)
    if has_conv:
        hp = _norm_mod(xp_ref[0], gamma, shift, scale)
        hn = _norm_mod(xn_ref[0], gamma, shift, scale)
        h_ref[0:SUBLANES, :] = jnp.where(i > 0, hp, 0.0).astype(BF16)
        h_ref[off + tm:off + tm + SUBLANES, :] = jnp.where(i < n_tiles - 1, hn, 0.0).astype(BF16)

    col = 0
    conv_col = 0
    for seg, o_ref in zip(segs, out_refs):
        width, epi, qscale = seg
        for c0 in range(0, width, chunk):
            wc = min(chunk, width - c0)
            w = w_ref[:, col + c0:col + c0 + wc]
            if epi == "conv_silu":
                u = jnp.dot(h_ref[...], w, preferred_element_type=F32)
                rows = tm + 2 * SUBLANES
                up = pltpu.roll(u, 1, 0)[off:off + tm]
                un = pltpu.roll(u, rows - 1, 0)[off:off + tm]
                uc = u[off:off + tm]
                cw = cw_ref[:, conv_col + c0:conv_col + c0 + wc]
                y = cw[0:1] * up + cw[1:2] * uc + cw[2:3] * un + cw[3:4]
                y = _silu(y)
            else:
                y = jnp.dot(h_ref[off:off + tm, :], w, preferred_element_type=F32)
                if qscale != 1.0:
                    y = y * qscale
                if epi == "rope":
                    cos = cos_ref[...]
                    sin = sin_ref[...]
                    y = jnp.concatenate(
                        [_rope_apply(y[:, k:k + LANES], cos, sin) for k in range(0, wc, LANES)], axis=1)
            o_ref[0, :, c0:c0 + wc] = y.astype(o_ref.dtype)
        col += width
        if epi == "conv_silu":
            conv_col += width


def _fused_proj(x, gamma, mod, w, segs, out_dtypes, *, tm, rope=None, conv_w=None, name):
    b, l, d = x.shape
    n_tiles = l // tm
    has_conv = conv_w is not None
    has_rope = rope is not None
    per_batch = mod.shape[0] > 1
    n_total = w.shape[1]
    bpt = tm // SUBLANES
    nblk8 = l // SUBLANES

    in_specs = [pl.BlockSpec((1, tm, d), lambda bi, i: (bi, i, 0))]
    args = [x]
    if has_conv:
        in_specs.append(pl.BlockSpec((1, SUBLANES, d), lambda bi, i: (bi, jnp.maximum(i * bpt - 1, 0), 0)))
        in_specs.append(pl.BlockSpec((1, SUBLANES, d), lambda bi, i: (bi, jnp.minimum((i + 1) * bpt, nblk8 - 1), 0)))
        args += [x, x]
    in_specs.append(pl.BlockSpec((1, d), lambda bi, i: (0, 0)))
    args.append(gamma.reshape(1, d))
    in_specs.append(pl.BlockSpec((1, 8, d), (lambda bi, i: (bi, 0, 0)) if per_batch else (lambda bi, i: (0, 0, 0))))
    args.append(mod)
    in_specs.append(pl.BlockSpec((d, n_total), lambda bi, i: (0, 0)))
    args.append(w)
    if has_rope:
        in_specs.append(pl.BlockSpec((tm, LANES), lambda bi, i: (i, 0)))
        in_specs.append(pl.BlockSpec((tm, LANES), lambda bi, i: (i, 0)))
        args += [rope[0], rope[1]]
    if has_conv:
        in_specs.append(pl.BlockSpec(conv_w.shape, lambda bi, i: (0, 0)))
        args.append(conv_w)

    out_specs = [pl.BlockSpec((1, tm, s[0]), lambda bi, i: (bi, i, 0)) for s in segs]
    out_shape = [jax.ShapeDtypeStruct((b, l, s[0]), dt) for s, dt in zip(segs, out_dtypes)]
    hrows = tm + (2 * SUBLANES if has_conv else 0)
    kern = functools.partial(_fused_proj_kernel, tm=tm, segs=tuple(segs), has_conv=has_conv,
                             has_rope=has_rope, n_tiles=n_tiles, chunk=512)
    return pl.pallas_call(
        kern,
        grid=(b, n_tiles),
        in_specs=in_specs,
        out_specs=out_specs,
        out_shape=out_shape,
        scratch_shapes=[pltpu.VMEM((hrows, d), BF16)],
        compiler_params=_cparams(("parallel", "parallel")),
        name=name,
    )(*args)


def _out_proj_kernel(o_ref, w_ref, x_ref, mod_ref, out_ref, *, gate_row):
    acc = jnp.dot(o_ref[0].astype(BF16), w_ref[...], preferred_element_type=F32)
    out_ref[0] = x_ref[0] + mod_ref[0, gate_row:gate_row + 1, :] * acc


def _mod_spec(mod):
    if mod.shape[0] > 1:
        return pl.BlockSpec((1, 8, D_MODEL), lambda bi, i: (bi, 0, 0))
    return pl.BlockSpec((1, 8, D_MODEL), lambda bi, i: (0, 0, 0))


def _out_proj(o, w, x, mod, *, tm, name):
    b, l, k = o.shape
    return pl.pallas_call(
        functools.partial(_out_proj_kernel, gate_row=2),
        grid=(b, l // tm),
        in_specs=[
            pl.BlockSpec((1, tm, k), lambda bi, i: (bi, i, 0)),
            pl.BlockSpec((k, D_MODEL), lambda bi, i: (0, 0)),
            pl.BlockSpec((1, tm, D_MODEL), lambda bi, i: (bi, i, 0)),
            _mod_spec(mod),
        ],
        out_specs=pl.BlockSpec((1, tm, D_MODEL), lambda bi, i: (bi, i, 0)),
        out_shape=jax.ShapeDtypeStruct((b, l, D_MODEL), F32),
        compiler_params=_cparams(("parallel", "parallel")),
        name=name,
    )(o, w, x, mod)


def _ffn_kernel(x_ref, xp_ref, xn_ref, g_ref, mod_ref, wu_ref, cw_ref, wd_ref, fg_ref,
                out_ref, h_ref, hnat_ref, acc_ref, act_ref, *, tm, n_tiles, final_norm):
    i = pl.program_id(1)
    gamma = g_ref[...]
    shift = mod_ref[0, 3:4, :]
    scale = mod_ref[0, 4:5, :]
    nv = tm // SUBLANES
    pitch = nv + SUBLANES
    rows = tm + 2 * SUBLANES
    n_slab = D_MODEL // LANES

    h_nat = _norm_mod(x_ref[0], gamma, shift, scale)
    for s in range(SUBLANES):
        for k in range(n_slab):
            hnat_ref[k, s * pitch:s * pitch + nv, :] = h_nat[s * nv:(s + 1) * nv, k * LANES:(k + 1) * LANES]

    def gather_rows(ref, start, stride):
        return jnp.concatenate([ref[k, pl.ds(start, SUBLANES, stride=stride), :] for k in range(n_slab)], axis=1)

    for v in range(0, nv, 2):
        pair = jnp.concatenate([gather_rows(hnat_ref, v, pitch), gather_rows(hnat_ref, v + 1, pitch)], axis=0)
        h_ref[v * SUBLANES:(v + 2) * SUBLANES, :] = pair.astype(BF16)
    hp = jnp.where(i > 0, _norm_mod(xp_ref[0], gamma, shift, scale), 0.0)
    hn = jnp.where(i < n_tiles - 1, _norm_mod(xn_ref[0], gamma, shift, scale), 0.0)
    h_ref[tm:rows, :] = jnp.concatenate([hp, hn], axis=0).astype(BF16)

    sub = lax.broadcasted_iota(jnp.int32, (SUBLANES, 1), 0)

    def conv(u, cw):
        main = u[0:tm]
        first_prev = pltpu.roll(jnp.where(sub == SUBLANES - 1, u[tm:tm + SUBLANES], u[tm - SUBLANES:tm]), 1, 0)
        last_next = pltpu.roll(jnp.where(sub == 0, u[tm + SUBLANES:rows], u[0:SUBLANES]), SUBLANES - 1, 0)
        prev = jnp.concatenate([first_prev, u[0:tm - SUBLANES]], axis=0)
        nxt = jnp.concatenate([u[SUBLANES:tm], last_next], axis=0)
        return cw[0:1] * prev + cw[1:2] * main + cw[2:3] * nxt + cw[3:4]

    n_chunks = D_FF // FF_CHUNK
    group_start = 0
    for c in range(n_chunks):
        gs = slice(c * FF_CHUNK, (c + 1) * FF_CHUNK)
        vs = slice(D_FF + c * FF_CHUNK, D_FF + (c + 1) * FF_CHUNK)
        h = h_ref[...]
        ug = jnp.dot(h, wu_ref[:, gs], preferred_element_type=F32)
        uv = jnp.dot(h, wu_ref[:, vs], preferred_element_type=F32)
        a = _silu(conv(ug, cw_ref[:, gs])) * conv(uv, cw_ref[:, vs])
        act_ref[:, gs] = a.astype(BF16)
        if (c + 1) % FF_GROUP == 0 or c == n_chunks - 1:
            ks = slice(group_start * FF_CHUNK, (c + 1) * FF_CHUNK)
            part = jnp.dot(act_ref[:, ks], wd_ref[ks, :], preferred_element_type=F32)
            for k in range(n_slab):
                if group_start == 0:
                    acc_ref[k] = part[:, k * LANES:(k + 1) * LANES]
                else:
                    acc_ref[k] += part[:, k * LANES:(k + 1) * LANES]
            group_start = c + 1
    gate = mod_ref[0, 5:6, :]
    for j in range(nv):
        s, v0 = divmod(j * SUBLANES, nv)
        rs = slice(j * SUBLANES, (j + 1) * SUBLANES)
        y = x_ref[0, rs, :] + gate * gather_rows(acc_ref, v0 * SUBLANES + s, SUBLANES)
        if final_norm:
            ms = jnp.mean(y * y, axis=-1, keepdims=True)
            y = (y * lax.rsqrt(ms + EPS)) * fg_ref[...]
        out_ref[0, rs, :] = y


def _ffn(x, gamma, mod, w_up, cw, w_down, final_gamma, *, tm, final_norm, name):
    b, l, d = x.shape
    n_tiles = l // tm
    bpt = tm // SUBLANES
    nblk8 = l // SUBLANES
    const2 = lambda bi, i: (0, 0)
    rows = tm + 2 * SUBLANES
    kern = functools.partial(_ffn_kernel, tm=tm, n_tiles=n_tiles, final_norm=final_norm)
    return pl.pallas_call(
        kern,
        grid=(b, n_tiles),
        in_specs=[
            pl.BlockSpec((1, tm, d), lambda bi, i: (bi, i, 0)),
            pl.BlockSpec((1, SUBLANES, d), lambda bi, i: (bi, jnp.maximum(i * bpt - 1, 0), 0)),
            pl.BlockSpec((1, SUBLANES, d), lambda bi, i: (bi, jnp.minimum((i + 1) * bpt, nblk8 - 1), 0)),
            pl.BlockSpec((1, d), const2),
            _mod_spec(mod),
            pl.BlockSpec(w_up.shape, const2),
            pl.BlockSpec(cw.shape, const2),
            pl.BlockSpec(w_down.shape, const2),
            pl.BlockSpec((1, d), const2),
        ],
        out_specs=pl.BlockSpec((1, tm, d), lambda bi, i: (bi, i, 0)),
        out_shape=jax.ShapeDtypeStruct((b, l, d), F32),
        scratch_shapes=[pltpu.VMEM((rows, d), BF16),
                        pltpu.VMEM((d // LANES, SUBLANES * (tm // SUBLANES + SUBLANES), LANES), F32),
                        pltpu.VMEM((d // LANES, tm, LANES), F32),
                        pltpu.VMEM((tm, D_FF), BF16)],
        compiler_params=_cparams(("parallel", "parallel")),
        name=name,
    )(x, x, x, gamma.reshape(1, d), mod, w_up, cw, w_down, final_gamma.reshape(1, d))


def _gqa_core(sink_ref, q_ref, kcat, vcat, bias, o_ref, tq):
    lane_head = lax.broadcasted_iota(jnp.int32, (1, WA_KV_HEADS * WA_HEAD_DIM), 1) // WA_HEAD_DIM
    width = WA_KV_HEADS * WA_HEAD_DIM

    def scores(g):
        qg = q_ref[0, :, g * width:(g + 1) * width]
        q4 = jnp.concatenate([jnp.where(lane_head == h, qg, jnp.zeros_like(qg)) for h in range(WA_KV_HEADS)], axis=0)
        return _dot_nt(q4, kcat)

    s_next = scores(0)
    for g in range(WA_GROUP):
        s = s_next
        if g + 1 < WA_GROUP:
            s_next = scores(g + 1)
        if bias is not None:
            s = s + jnp.concatenate([bias] * WA_KV_HEADS, axis=0)
        sk = jnp.concatenate([jnp.full((tq, 1), sink_ref[h * WA_GROUP + g], F32) for h in range(WA_KV_HEADS)], axis=0)
        m = jnp.maximum(jnp.max(s, axis=-1, keepdims=True), sk)
        p = jnp.exp(s - m)
        denom = jnp.sum(p, axis=-1, keepdims=True) + jnp.exp(sk - m)
        o4 = jnp.dot(p.astype(BF16), vcat, preferred_element_type=F32) / denom
        og = jnp.zeros((tq, width), F32)
        for h in range(WA_KV_HEADS):
            og = og + jnp.where(lane_head == h, o4[h * tq:(h + 1) * tq], 0.0)
        o_ref[0, :, g * width:(g + 1) * width] = og.astype(o_ref.dtype)


def _win_ctx_kernel(sink_ref, q_ref, k_ref, v_ref, o_ref, *, tq):
    _gqa_core(sink_ref, q_ref, k_ref[0].astype(BF16), v_ref[0].astype(BF16), None, o_ref, tq)


def _win_ctx_attn(q, k, v, sink, *, tq):
    b, l, _ = q.shape
    kvw = WA_KV_HEADS * WA_HEAD_DIM
    return pl.pallas_call(
        functools.partial(_win_ctx_kernel, tq=tq),
        grid=(b, l // tq),
        in_specs=[
            pl.BlockSpec(memory_space=pltpu.SMEM),
            pl.BlockSpec((1, tq, D_MODEL), lambda bi, i: (bi, i, 0)),
            pl.BlockSpec((1, l, kvw), lambda bi, i: (bi, 0, 0)),
            pl.BlockSpec((1, l, kvw), lambda bi, i: (bi, 0, 0)),
        ],
        out_specs=pl.BlockSpec((1, tq, D_MODEL), lambda bi, i: (bi, i, 0)),
        out_shape=jax.ShapeDtypeStruct((b, l, D_MODEL), BF16),
        compiler_params=_cparams(("parallel", "parallel")),
        name="win_ctx_attn",
    )(sink, q, k, v)


def _win_lat_kernel(sink_ref, q_ref, kp_ref, kc_ref, kn_ref, vp_ref, vc_ref, vn_ref, kx_ref, vx_ref, o_ref,
                    kcat, vcat, *, tq, seq_len, n_ctx):
    i = pl.program_id(1)
    halo = WINDOW
    n_lat = tq + 2 * halo
    nk = n_lat + n_ctx
    for src, dst in ((kp_ref, kcat), (vp_ref, vcat)):
        dst[0:halo] = src[0]
    for src, dst in ((kc_ref, kcat), (vc_ref, vcat)):
        dst[halo:halo + tq] = src[0]
    for src, dst in ((kn_ref, kcat), (vn_ref, vcat)):
        dst[halo + tq:n_lat] = src[0]
    for src, dst in ((kx_ref, kcat), (vx_ref, vcat)):
        dst[n_lat:nk] = src[0]
    r = lax.broadcasted_iota(jnp.int32, (tq, nk), 0)
    c = lax.broadcasted_iota(jnp.int32, (tq, nk), 1)
    kpos = i * tq - halo + c
    ok = (c >= n_lat) | ((kpos >= 0) & (kpos < seq_len) & (jnp.abs(r + halo - c) <= WINDOW))
    bias = jnp.where(ok, 0.0, NEG_BIG).astype(F32)
    _gqa_core(sink_ref, q_ref, kcat[...], vcat[...], bias, o_ref, tq)


def _win_lat_attn(q, k, v, kctx, vctx, sink, *, tq):
    b, l, _ = q.shape
    halo = WINDOW
    assert tq % halo == 0 and l % tq == 0
    nq = l // tq
    per = tq // halo
    n_halo_blocks = l // halo
    n_ctx = kctx.shape[1]
    kvw = WA_KV_HEADS * WA_HEAD_DIM
    n_lat = tq + 2 * halo
    prev = lambda bi, i: (bi, jnp.maximum(i * per - 1, 0), 0)
    cur = lambda bi, i: (bi, i, 0)
    nxt = lambda bi, i: (bi, jnp.minimum((i + 1) * per, n_halo_blocks - 1), 0)
    halo_spec = lambda f: pl.BlockSpec((1, halo, kvw), f)
    cur_spec = pl.BlockSpec((1, tq, kvw), cur)
    return pl.pallas_call(
        functools.partial(_win_lat_kernel, tq=tq, seq_len=l, n_ctx=n_ctx),
        grid=(b, nq),
        in_specs=[
            pl.BlockSpec(memory_space=pltpu.SMEM),
            pl.BlockSpec((1, tq, D_MODEL), cur),
            halo_spec(prev), cur_spec, halo_spec(nxt),
            halo_spec(prev), cur_spec, halo_spec(nxt),
            pl.BlockSpec((1, n_ctx, kvw), lambda bi, i: (bi, 0, 0)),
            pl.BlockSpec((1, n_ctx, kvw), lambda bi, i: (bi, 0, 0)),
        ],
        out_specs=pl.BlockSpec((1, tq, D_MODEL), cur),
        out_shape=jax.ShapeDtypeStruct((b, l, D_MODEL), BF16),
        scratch_shapes=[pltpu.VMEM((n_lat + n_ctx, kvw), BF16), pltpu.VMEM((n_lat + n_ctx, kvw), BF16)],
        compiler_params=_cparams(("parallel", "parallel")),
        name="win_lat_attn",
    )(sink, q, k, k, k, v, v, v, kctx, vctx)


def _diff_kernel(lqk_ref, gsub_ref, q_ref, k_ref, v_ref, *rest, tq, kc, n_keys, has_ctx, lam_init):
    if has_ctx:
        kx_ref, vx_ref, o_ref, kall, vall, m_ref, acc_ref, s_a, s_b = rest
    else:
        o_ref, kall, vall, m_ref, acc_ref, s_a, s_b = rest
    hd = DA_HEAD_DIM
    lam = (jnp.exp(jnp.sum(lqk_ref[0:1, :] * lqk_ref[1:2, :], axis=-1, keepdims=True))
           - jnp.exp(jnp.sum(lqk_ref[2:3, :] * lqk_ref[3:4, :], axis=-1, keepdims=True)) + lam_init)
    hw = 2 * hd

    @pl.when(pl.program_id(2) == 0)
    def _():
        lat = k_ref.shape[1]
        kall[0:lat, :] = k_ref[0].astype(BF16)
        vall[0:lat, 0:hw] = v_ref[0].astype(BF16)
        if has_ctx:
            kall[lat:n_keys, :] = kx_ref[0]
            vall[lat:n_keys, 0:hw] = vx_ref[0]
        vall[:, hw:2 * hw] = jnp.ones((n_keys, hw), BF16)

    q = q_ref[0]
    lane = lax.broadcasted_iota(jnp.int32, (1, hw), 1)
    zero = jnp.zeros_like(q)
    q2 = jnp.concatenate([jnp.where(lane < hd, q, zero), jnp.where(lane >= hd, q, zero)], axis=0)

    n_chunks = n_keys // kc
    s_bufs = (s_a, s_b)
    s_bufs[0][...] = _dot_nt(q2, kall[0:kc, :])
    for j in range(n_chunks):
        if j + 1 < n_chunks:
            s_bufs[(j + 1) % 2][...] = _dot_nt(q2, kall[(j + 1) * kc:(j + 2) * kc, :])
        s = s_bufs[j % 2][...]
        m_chunk = jnp.max(s, axis=-1, keepdims=True)
        if j == 0:
            m_new = jnp.broadcast_to(m_chunk, m_ref.shape)
        else:
            m_old = m_ref[...]
            m_new = jnp.maximum(m_old, m_chunk)
            alpha = jnp.exp(m_old - m_new)
        p = jnp.exp(s - jnp.concatenate([m_new] * (kc // LANES), axis=1))
        pv = jnp.dot(p.astype(BF16), vall[j * kc:(j + 1) * kc, :], preferred_element_type=F32)
        if j == 0:
            acc_ref[...] = pv
        else:
            acc_ref[...] = jnp.concatenate([alpha, alpha], axis=1) * acc_ref[...] + pv
        if j + 1 < n_chunks:
            m_ref[...] = m_new
    acc = acc_ref[...]
    o2 = acc[:, 0:hw] / acc[:, hw:2 * hw]
    o = o2[0:tq] - lam * o2[tq:2 * tq]
    ms = jnp.mean(o * o, axis=-1, keepdims=True)
    o = (o * lax.rsqrt(ms + EPS)) * gsub_ref[...] * (1.0 - lam_init)
    o_ref[0] = o.astype(o_ref.dtype)


def _diff_attn(q, k, v, kctx, vctx, lqk, gsub, *, tq, kc, lam_init):
    b, l, _ = q.shape
    hw = 2 * DA_HEAD_DIM
    has_ctx = kctx is not None
    in_specs = [
        pl.BlockSpec((8, hw), lambda bi, h, i: (0, 0)),
        pl.BlockSpec((1, hw), lambda bi, h, i: (0, 0)),
        pl.BlockSpec((1, tq, hw), lambda bi, h, i: (bi, i, h)),
        pl.BlockSpec((1, l, hw), lambda bi, h, i: (bi, 0, h)),
        pl.BlockSpec((1, l, hw), lambda bi, h, i: (bi, 0, h)),
    ]
    args = [lqk, gsub, q, k, v]
    n_keys = l
    if has_ctx:
        n_ctx = kctx.shape[1]
        n_keys = l + n_ctx
        in_specs.append(pl.BlockSpec((1, n_ctx, hw), lambda bi, h, i: (bi, 0, h)))
        in_specs.append(pl.BlockSpec((1, n_ctx, hw), lambda bi, h, i: (bi, 0, h)))
        args += [kctx, vctx]
    assert n_keys % kc == 0 and kc % LANES == 0
    kern = functools.partial(_diff_kernel, tq=tq, kc=kc, n_keys=n_keys, has_ctx=has_ctx, lam_init=lam_init)
    return pl.pallas_call(
        kern,
        grid=(b, DA_HEADS, l // tq),
        in_specs=in_specs,
        out_specs=pl.BlockSpec((1, tq, hw), lambda bi, h, i: (bi, i, h)),
        out_shape=jax.ShapeDtypeStruct((b, l, D_MODEL), BF16),
        scratch_shapes=[pltpu.VMEM((n_keys, hw), BF16), pltpu.VMEM((n_keys, 2 * hw), BF16),
                        pltpu.VMEM((2 * tq, LANES), F32), pltpu.VMEM((2 * tq, 2 * hw), F32),
                        pltpu.VMEM((2 * tq, kc), F32), pltpu.VMEM((2 * tq, kc), F32)],
        compiler_params=_cparams(("parallel", "parallel", "arbitrary")),
        name="diff_lat_attn" if has_ctx else "diff_ctx_attn",
    )(*args)


def _gla_scan_kernel(qf_ref, kf_ref, vf_ref, zf_ref, qb_ref, kb_ref, vb_ref, zb_ref, w2_ref, b2_ref, s0_ref,
                     of_ref, ob_ref, sf_ref, st_scr, qd_scr, kd_scr, kr_scr, vt_scr, mt_scr, el_scr,
                     *, blk, n_blocks):
    i = pl.program_id(1)
    ch = GLA_CHUNK
    nk = GLA_HEADS * GLA_DK
    n_sub = blk // ch
    dirs = ((qf_ref, kf_ref, vf_ref, zf_ref, of_ref), (qb_ref, kb_ref, vb_ref, zb_ref, ob_ref))

    @pl.when(i == 0)
    def _():
        for d in range(2):
            for h in range(GLA_HEADS):
                st_scr[d, h] = jnp.transpose(s0_ref[d, 0, h])

    cs = min(blk, GLA_CUM_ROWS)
    r = lax.broadcasted_iota(jnp.int32, (cs, cs), 0)
    c = lax.broadcasted_iota(jnp.int32, (cs, cs), 1)
    same_chunk = (r // ch) == (c // ch)
    rr = lax.broadcasted_iota(jnp.int32, (ch, ch), 0)
    cc = lax.broadcasted_iota(jnp.int32, (ch, ch), 1)
    keeps = (rr >= cc, rr <= cc)
    edges = (ch - 1, 0)

    for d, (q_ref, k_ref, v_ref, zl_ref, _) in enumerate(dirs):
        tri = (same_chunk & ((r <= c) if d == 1 else (r >= c))).astype(BF16)
        z = jnp.dot(zl_ref[0].astype(BF16), w2_ref[d], preferred_element_type=F32) + b2_ref[d]
        logg = (jnp.minimum(z, 0.0) - jnp.log1p(jnp.exp(-jnp.abs(z)))) / GLA_TAU
        cum = jnp.concatenate([_exact_dot_left01(tri, logg[t:t + cs]) for t in range(0, blk, cs)], axis=0)
        lasts = [cum[s * ch + edges[d]:s * ch + edges[d] + 1, :] for s in range(n_sub)]
        blast = jnp.concatenate([jnp.broadcast_to(b, (ch, nk)) for b in lasts], axis=0)
        k = k_ref[0]
        qd_scr[d] = ((q_ref[0] * (GLA_DK ** -0.5)) * jnp.exp(cum)).astype(BF16)
        kd_scr[d] = (k * jnp.exp(-cum)).astype(BF16)
        kr_scr[d] = (k * jnp.exp(blast - cum)).astype(BF16)
        for s in range(n_sub):
            el_scr[d, s:s + 1, :] = jnp.exp(lasts[s])
        vt_scr[d] = jnp.transpose(v_ref[0].astype(F32))

    lane_chunk = lax.broadcasted_iota(jnp.int32, (1, 2 * ch), 1) // ch
    for s in range(n_sub):
        rows = slice(s * ch, (s + 1) * ch)
        pair = slice((s // 2) * 2 * ch, (s // 2 + 1) * 2 * ch)
        for h in range(GLA_HEADS):
            ks = slice(h * GLA_DK, (h + 1) * GLA_DK)
            vs = slice(h * GLA_DV, (h + 1) * GLA_DV)
            for d, (_, _, v_ref, _, o_ref) in enumerate(dirs):
                att = jnp.where(keeps[d], _dot_nt(qd_scr[d, rows, ks], kd_scr[d, rows, ks]), 0.0)
                o_ref[0, rows, vs] = jnp.dot(att.astype(BF16), v_ref[0, rows, vs].astype(BF16),
                                             preferred_element_type=F32)
                vt = jnp.where(lane_chunk == s % 2, vt_scr[d, vs, pair], 0.0).astype(BF16)
                mt_scr[d, s, h] = jnp.dot(vt, kr_scr[d, pair, ks], preferred_element_type=F32)

    for step in range(n_sub):
        for h in range(GLA_HEADS):
            ks = slice(h * GLA_DK, (h + 1) * GLA_DK)
            vs = slice(h * GLA_DV, (h + 1) * GLA_DV)
            for d, (_, _, _, _, o_ref) in enumerate(dirs):
                s = step if d == 0 else n_sub - 1 - step
                rows = slice(s * ch, (s + 1) * ch)
                st = st_scr[d, h]
                o_ref[0, rows, vs] += _dot_nt(qd_scr[d, rows, ks], st.astype(BF16))
                st_scr[d, h] = st * el_scr[d, s:s + 1, ks] + mt_scr[d, s, h]

    @pl.when(i == n_blocks - 1)
    def _():
        for d in range(2):
            for h in range(GLA_HEADS):
                sf_ref[d, 0, h] = jnp.transpose(st_scr[d, h])


def _gla_scan(q, k, v, zl, w2, b2, s0, *, blk):
    b, l, nk = q.shape
    nv = v.shape[-1]
    n_blocks = l // blk
    n_sub = blk // GLA_CHUNK
    assert blk % (2 * GLA_CHUNK) == 0 and n_sub <= SUBLANES
    fwd = lambda bi, i: (bi, i, 0)
    bwd = lambda bi, i: (bi, n_blocks - 1 - i, 0)
    row_specs = lambda f: [pl.BlockSpec((1, blk, nk), f), pl.BlockSpec((1, blk, nk), f),
                           pl.BlockSpec((1, blk, nv), f), pl.BlockSpec((1, blk, LANES), f)]
    state_spec = pl.BlockSpec((2, 1, GLA_HEADS, GLA_DK, GLA_DV), lambda bi, i: (0, bi, 0, 0, 0))
    return pl.pallas_call(
        functools.partial(_gla_scan_kernel, blk=blk, n_blocks=n_blocks),
        grid=(b, n_blocks),
        in_specs=row_specs(fwd) + row_specs(bwd) + [
            pl.BlockSpec((2, LANES, nk), lambda bi, i: (0, 0, 0)),
            pl.BlockSpec((2, 1, nk), lambda bi, i: (0, 0, 0)),
            state_spec,
        ],
        out_specs=[pl.BlockSpec((1, blk, nv), fwd), pl.BlockSpec((1, blk, nv), bwd), state_spec],
        out_shape=[jax.ShapeDtypeStruct((b, l, nv), F32), jax.ShapeDtypeStruct((b, l, nv), F32),
                   jax.ShapeDtypeStruct((2, b, GLA_HEADS, GLA_DK, GLA_DV), F32)],
        scratch_shapes=[pltpu.VMEM((2, GLA_HEADS, GLA_DV, GLA_DK), F32),
                        pltpu.VMEM((2, blk, nk), BF16), pltpu.VMEM((2, blk, nk), BF16),
                        pltpu.VMEM((2, blk, nk), BF16),
                        pltpu.VMEM((2, nv, blk), F32),
                        pltpu.VMEM((2, n_sub, GLA_HEADS, GLA_DV, GLA_DK), F32),
                        pltpu.VMEM((2, SUBLANES, nk), F32)],
        compiler_params=_cparams(("parallel", "arbitrary")),
        name="gla_scan",
    )(q, k, v, zl, q, k, v, zl, w2, b2, s0)


def _gla_out_kernel(of_ref, ob_ref, r_ref, gh_ref, w_ref, x_ref, mod_ref, out_ref):
    o = of_ref[0] + ob_ref[0]
    parts = []
    for h in range(GLA_HEADS):
        oh = o[:, h * GLA_DV:(h + 1) * GLA_DV]
        ms = jnp.mean(oh * oh, axis=-1, keepdims=True)
        parts.append((oh * lax.rsqrt(ms + EPS)) * gh_ref[...])
    y = jnp.concatenate(parts, axis=1) * _silu(r_ref[0].astype(F32))
    acc = jnp.dot(y.astype(BF16), w_ref[...], preferred_element_type=F32)
    out_ref[0] = x_ref[0] + mod_ref[0, 2:3, :] * acc


def _gla_out(o_f, o_b, r, g_head, w, x, mod, *, tm):
    b, l, nv = o_f.shape
    row = lambda bi, i: (bi, i, 0)
    return pl.pallas_call(
        _gla_out_kernel,
        grid=(b, l // tm),
        in_specs=[
            pl.BlockSpec((1, tm, nv), row),
            pl.BlockSpec((1, tm, nv), row),
            pl.BlockSpec((1, tm, nv), row),
            pl.BlockSpec((1, GLA_DV), lambda bi, i: (0, 0)),
            pl.BlockSpec((nv, D_MODEL), lambda bi, i: (0, 0)),
            pl.BlockSpec((1, tm, D_MODEL), row),
            _mod_spec(mod),
        ],
        out_specs=pl.BlockSpec((1, tm, D_MODEL), row),
        out_shape=jax.ShapeDtypeStruct((b, l, D_MODEL), F32),
        compiler_params=_cparams(("parallel", "parallel")),
        name="gla_out",
    )(o_f, o_b, r, g_head.reshape(1, GLA_DV), w, x, mod)


def _ssd_scan_kernel(x_ref, b_ref, c_ref, dt_ref, bias_ref, a_ref, dsk_ref, s0_ref, y_ref, sf_ref, s_scr, xt_scr,
                     *, reverse, n_chunks, lane_off, add_skip):
    i = pl.program_id(1)
    ch = SSD_CHUNK
    p = SSD_HEAD_DIM
    hpg = SSD_HEADS // SSD_GROUPS

    @pl.when(i == 0)
    def _():
        s_scr[...] = s0_ref[0]

    r = lax.broadcasted_iota(jnp.int32, (ch, ch), 0)
    c = lax.broadcasted_iota(jnp.int32, (ch, ch), 1)
    keep = (r <= c) if reverse else (r >= c)
    tri = keep.astype(BF16)
    edge = 0 if reverse else ch - 1

    dt = _softplus(dt_ref[0] + bias_ref[...])
    la = dt * a_ref[...]
    cum = _exact_dot_left01(tri, la) * LOG2_E
    cum_t = jnp.transpose(cum)
    dt_t = jnp.transpose(dt)
    to_end_t = jnp.transpose(jnp.exp2(cum[edge:edge + 1, :] - cum) * dt)
    xt_scr[...] = jnp.transpose(x_ref[0])
    keep_f = keep.astype(F32)
    lane = lax.broadcasted_iota(jnp.int32, (1, LANES), 1)
    lo_half = lane < p

    for g in range(SSD_GROUPS):
        bg = b_ref[0, :, g * SSD_STATE:(g + 1) * SSD_STATE].astype(BF16)
        cg = c_ref[0, :, g * SSD_STATE:(g + 1) * SSD_STATE].astype(BF16)
        cb = _dot_nt(cg, bg) * keep_f
        gs = slice(g * hpg * p, (g + 1) * hpg * p)
        s_g = s_scr[gs, :]
        y_inter = _dot_nt(cg, s_g.astype(BF16))
        for pair in range(hpg // 2):
            cols = slice((g * hpg + 2 * pair) * p, (g * hpg + 2 * pair + 2) * p)
            x_pair = x_ref[0, :, cols]
            ws = []
            e_is = []
            for t in range(2):
                hl = lane_off + g * hpg + 2 * pair + t
                cum_i = jnp.broadcast_to(cum[:, hl:hl + 1], (ch, ch))
                seg = jnp.minimum(cum_i - cum_t[hl:hl + 1, :], 0.0)
                ws.append((cb * jnp.exp2(seg) * dt_t[hl:hl + 1, :]).astype(BF16))
                e_is.append(jnp.exp2(cum_i))
            xb = x_pair.astype(BF16)
            zero = jnp.zeros_like(xb)
            x_bd = jnp.concatenate([jnp.where(lo_half, xb, zero), jnp.where(lo_half, zero, xb)], axis=0)
            y_pair = jnp.dot(jnp.concatenate(ws, axis=1), x_bd, preferred_element_type=F32)
            y_pair = y_pair + y_inter[:, 2 * pair * p:(2 * pair + 2) * p] * jnp.where(lo_half, e_is[0], e_is[1])
            if add_skip:
                hl0 = g * hpg + 2 * pair
                dsk = jnp.where(lo_half, dsk_ref[:, hl0:hl0 + 1], dsk_ref[:, hl0 + 1:hl0 + 2])
                y_pair = y_pair + x_pair * dsk
            y_ref[0, :, cols] = y_pair.astype(y_ref.dtype)
        xs_t = []
        for hh in range(hpg):
            hl = lane_off + g * hpg + hh
            rs = slice((g * hpg + hh) * p, (g * hpg + hh + 1) * p)
            xs_t.append((xt_scr[rs, :] * to_end_t[hl:hl + 1, :]).astype(BF16))
        ds = jnp.dot(jnp.concatenate(xs_t, axis=0), bg, preferred_element_type=F32)
        for hh in range(hpg):
            hl = lane_off + g * hpg + hh
            tot = jnp.exp2(cum_t[hl:hl + 1, edge:edge + 1])
            rs = slice((g * hpg + hh) * p, (g * hpg + hh + 1) * p)
            s_scr[rs, :] = s_scr[rs, :] * tot + ds[hh * p:(hh + 1) * p, :]

    @pl.when(i == n_chunks - 1)
    def _():
        sf_ref[0] = s_scr[...]


def _ssd_scan(x, bc, dt, bias, a, dskip, s0, *, reverse, add_skip):
    b, l, _ = x.shape
    ch = SSD_CHUNK
    n_chunks = l // ch
    gn = SSD_GROUPS * SSD_STATE
    rows = SSD_HEADS * SSD_HEAD_DIM

    def at(col):
        if reverse:
            return lambda bi, i: (bi, n_chunks - 1 - i, col)
        return lambda bi, i: (bi, i, col)

    vec = pl.BlockSpec((1, LANES), lambda bi, i: (0, 0))
    state_spec = pl.BlockSpec((1, rows, SSD_STATE), lambda bi, i: (bi, 0, 0))
    kern = functools.partial(_ssd_scan_kernel, reverse=reverse, n_chunks=n_chunks,
                             lane_off=SSD_HEADS if reverse else 0, add_skip=add_skip)
    return pl.pallas_call(
        kern,
        grid=(b, n_chunks),
        in_specs=[
            pl.BlockSpec((1, ch, SSD_D_INNER), at(0)),
            pl.BlockSpec((1, ch, gn), at(0)),
            pl.BlockSpec((1, ch, gn), at(1)),
            pl.BlockSpec((1, ch, LANES), at(0)),
            vec, vec, vec,
            state_spec,
        ],
        out_specs=[pl.BlockSpec((1, ch, SSD_D_INNER), at(0)), state_spec],
        out_shape=[jax.ShapeDtypeStruct((b, l, SSD_D_INNER), BF16),
                   jax.ShapeDtypeStruct((b, rows, SSD_STATE), F32)],
        scratch_shapes=[pltpu.VMEM((rows, SSD_STATE), F32), pltpu.VMEM((SSD_D_INNER, ch), F32)],
        compiler_params=_cparams(("parallel", "arbitrary")),
        name="ssd_scan_bwd" if reverse else "ssd_scan_fwd",
    )(x, bc, bc, dt, bias, a, dskip, s0)


def _ssd_out_kernel(yf_ref, yb_ref, z_ref, gn_ref, w_ref, x_ref, mod_ref, out_ref):
    y = (yf_ref[0].astype(F32) + yb_ref[0].astype(F32)) * _silu(z_ref[0].astype(F32))
    ms = jnp.mean(y * y, axis=-1, keepdims=True)
    y = (y * lax.rsqrt(ms + EPS)) * gn_ref[...]
    acc = jnp.dot(y.astype(BF16), w_ref[...], preferred_element_type=F32)
    out_ref[0] = x_ref[0] + mod_ref[0, 2:3, :] * acc


def _ssd_out(y_f, y_b, z, g_norm, w, x, mod, *, tm):
    b, l, di = y_f.shape
    row = lambda bi, i: (bi, i, 0)
    return pl.pallas_call(
        _ssd_out_kernel,
        grid=(b, l // tm),
        in_specs=[
            pl.BlockSpec((1, tm, di), row),
            pl.BlockSpec((1, tm, di), row),
            pl.BlockSpec((1, tm, di), row),
            pl.BlockSpec((1, di), lambda bi, i: (0, 0)),
            pl.BlockSpec((di, D_MODEL), lambda bi, i: (0, 0)),
            pl.BlockSpec((1, tm, D_MODEL), row),
            _mod_spec(mod),
        ],
        out_specs=pl.BlockSpec((1, tm, D_MODEL), row),
        out_shape=jax.ShapeDtypeStruct((b, l, D_MODEL), F32),
        compiler_params=_cparams(("parallel", "parallel")),
        name="ssd_out",
    )(y_f, y_b, z, g_norm.reshape(1, di), w, x, mod)


def _rope_tables(n_tokens, dim):
    rows = n_tokens // GRID_W
    row = jnp.repeat(jnp.arange(rows, dtype=F32), GRID_W)
    col = jnp.tile(jnp.arange(GRID_W, dtype=F32), rows)
    axis_dim = dim // 2
    inv = ROPE_BASE ** (-jnp.arange(0, axis_dim, 2, dtype=F32) / axis_dim)
    ar = row[:, None] * inv
    ac = col[:, None] * inv
    cos = jnp.concatenate([jnp.cos(ar), jnp.cos(ar), jnp.cos(ac), jnp.cos(ac)], axis=1)
    sin = jnp.concatenate([-jnp.sin(ar), jnp.sin(ar), -jnp.sin(ac), jnp.sin(ac)], axis=1)
    reps = LANES // dim
    return jnp.tile(cos, (1, reps)), jnp.tile(sin, (1, reps))


def _pad_cols(w, width):
    return jnp.pad(w, ((0, 0), (0, width - w.shape[1])))


def _conv_pack(conv_w, conv_b):
    return jnp.concatenate([conv_w, conv_b[None], jnp.zeros((4, conv_w.shape[1]), F32)], axis=0)


def _tm_for(l):
    return min(l, 512)


def _key_chunk(n_keys):
    for kc in (1536, 768, 512, 256):
        if n_keys % kc == 0:
            return kc
    return n_keys


def kernel(x_prompt, x_sample, c, c_ctx, cache_win_k, cache_win_v, state_gla_fwd, state_gla_bwd, cache_diff_k, cache_diff_v, state_ssd_fwd, state_ssd_bwd, ada_w, ada_b, norm_mix, norm_ffn, ffn_w_up, ffn_conv_w, ffn_conv_b, ffn_w_down, final_norm, win_w_qkv, win_w_o, win_sink, gla_w_qkvr, gla_w_gf1, gla_w_gf2, gla_b_gf, gla_w_gb1, gla_w_gb2, gla_b_gb, gla_norm, gla_w_o, diff_w_qkv, diff_lq1, diff_lk1, diff_lq2, diff_lk2, diff_norm, diff_w_o, ssd_w_in, ssd_conv_w, ssd_conv_b, ssd_a_log_f, ssd_a_log_b, ssd_dt_bias_f, ssd_dt_bias_b, ssd_d, ssd_norm, ssd_w_out):
    xp, xs = x_prompt, x_sample
    bp, lp, d = xp.shape
    bs, ls, _ = xs.shape
    tmp, tms = _tm_for(lp), _tm_for(ls)

    n_cond = 1 + bs
    cond_rows = -(-n_cond // SUBLANES) * SUBLANES
    cond = jnp.concatenate([c_ctx[None], c, jnp.zeros((cond_rows - n_cond, d), F32)], axis=0)
    mods = _ada_call(cond, ada_w, ada_b).reshape(DEPTH, cond_rows, 6, d)
    mods = jnp.pad(mods, ((0, 0), (0, 0), (0, 2), (0, 0)))

    outs = {}
    for i in range(DEPTH):
        kind, j = i % 4, i // 4
        mod_p = mods[i, 0:1]
        mod_s = mods[i, 1:1 + bs]
        if kind == 0:
            nq = WA_HEADS * WA_HEAD_DIM
            nkv = WA_KV_HEADS * WA_HEAD_DIM
            perm = jnp.arange(nq).reshape(WA_KV_HEADS, WA_GROUP, WA_HEAD_DIM).transpose(1, 0, 2).reshape(-1)
            wq = win_w_qkv[j][:, :nq][:, perm]
            w = jnp.concatenate([wq, win_w_qkv[j][:, nq:]], axis=1).astype(BF16)
            w_o = win_w_o[j][perm, :].astype(BF16)
            sink = win_sink[j]
            qscale = WA_HEAD_DIM ** -0.5
            q, k, v = _fused_proj(xp, norm_mix[i], mod_p, w,
                                  [(nq, "plain", qscale), (nkv, "plain", 1.0), (nkv, "plain", 1.0)],
                                  [BF16, F32, F32], tm=tmp, name="win_proj_ctx")
            outs["win_k"] = k.reshape(bp, 1, lp, WA_KV_HEADS, WA_HEAD_DIM)
            outs["win_v"] = v.reshape(bp, 1, lp, WA_KV_HEADS, WA_HEAD_DIM)
            o = _win_ctx_attn(q, k, v, sink, tq=min(lp, 128))
            xp = _out_proj(o, w_o, xp, mod_p, tm=tmp, name="win_out_ctx")
            rope = _rope_tables(ls, WA_HEAD_DIM)
            q, k, v = _fused_proj(xs, norm_mix[i], mod_s, w,
                                  [(nq, "rope", qscale), (nkv, "rope", 1.0), (nkv, "plain", 1.0)],
                                  [BF16, BF16, BF16], tm=tms, rope=rope, name="win_proj_lat")
            n_ctx = cache_win_k.shape[2]
            kctx = cache_win_k[:, j].reshape(bs, n_ctx, nkv).astype(BF16)
            vctx = cache_win_v[:, j].reshape(bs, n_ctx, nkv).astype(BF16)
            o = _win_lat_attn(q, k, v, kctx, vctx, sink, tq=min(ls, WINDOW))
            xs = _out_proj(o, w_o, xs, mod_s, tm=tms, name="win_out_lat")
        elif kind == 1:
            nk = GLA_HEADS * GLA_DK
            nv = GLA_HEADS * GLA_DV
            w1 = _pad_cols(jnp.concatenate([gla_w_gf1[j], gla_w_gb1[j]], axis=1), LANES)
            w = jnp.concatenate([gla_w_qkvr[j], w1], axis=1).astype(BF16)
            segs = [(nk, "plain", 1.0), (nk, "plain", 1.0), (nv, "plain", 1.0), (nv, "plain", 1.0),
                    (LANES, "plain", 1.0)]
            zrows = jnp.zeros((LANES - 2 * GLA_RANK, nk), F32)
            w2_f = jnp.concatenate([gla_w_gf2[j], jnp.zeros((GLA_RANK, nk), F32), zrows], axis=0).astype(BF16)
            w2_b = jnp.concatenate([jnp.zeros((GLA_RANK, nk), F32), gla_w_gb2[j], zrows], axis=0).astype(BF16)
            w2 = jnp.stack([w2_f, w2_b], axis=0)
            b2 = jnp.stack([gla_b_gf[j], gla_b_gb[j]], axis=0).reshape(2, 1, nk)
            w_o = gla_w_o[j].astype(BF16)
            dtypes = [F32, F32, BF16, BF16, F32]
            for stream in ("p", "s"):
                if stream == "p":
                    x, mod, tm = xp, mod_p, tmp
                    s0 = jnp.zeros((2, bp, GLA_HEADS, GLA_DK, GLA_DV), F32)
                else:
                    x, mod, tm = xs, mod_s, tms
                    s0 = jnp.stack([state_gla_fwd[:, j], state_gla_bwd[:, j]], axis=0)
                q, k, v, r, zl = _fused_proj(x, norm_mix[i], mod, w, segs, dtypes, tm=tm,
                                             name="gla_proj_" + stream)
                o_f, o_b, s_fb = _gla_scan(q, k, v, zl, w2, b2, s0, blk=min(x.shape[1], 512))
                x = _gla_out(o_f, o_b, r, gla_norm[j], w_o, x, mod, tm=tm)
                if stream == "p":
                    xp = x
                    outs["gla_f"] = s_fb[0][:, None]
                    outs["gla_b"] = s_fb[1][:, None]
                else:
                    xs = x
        elif kind == 2:
            lam_init = 0.8 - 0.6 * math.exp(-0.3 * i)
            nh = DA_HEADS * 2 * DA_HEAD_DIM
            w = diff_w_qkv[j].astype(BF16)
            w_o = diff_w_o[j].astype(BF16)
            qscale = DA_HEAD_DIM ** -0.5
            lqk = jnp.stack([diff_lq1[j], diff_lk1[j], diff_lq2[j], diff_lk2[j]], axis=0)
            lqk = jnp.pad(lqk, ((0, 4), (0, 2 * DA_HEAD_DIM - lqk.shape[1])))
            gsub = diff_norm[j].reshape(1, 2 * DA_HEAD_DIM)
            q, k, v = _fused_proj(xp, norm_mix[i], mod_p, w,
                                  [(nh, "plain", qscale), (nh, "plain", 1.0), (nh, "plain", 1.0)],
                                  [BF16, F32, F32], tm=tmp, name="diff_proj_ctx")
            outs["diff_k"] = k.reshape(bp, 1, lp, DA_HEADS, 2, DA_HEAD_DIM)
            outs["diff_v"] = v.reshape(bp, 1, lp, DA_HEADS, 2 * DA_HEAD_DIM)
            o = _diff_attn(q, k, v, None, None, lqk, gsub, tq=min(lp, 256), kc=_key_chunk(lp), lam_init=lam_init)
            xp = _out_proj(o, w_o, xp, mod_p, tm=tmp, name="diff_out_ctx")
            rope = _rope_tables(ls, DA_HEAD_DIM)
            q, k, v = _fused_proj(xs, norm_mix[i], mod_s, w,
                                  [(nh, "rope", qscale), (nh, "rope", 1.0), (nh, "plain", 1.0)],
                                  [BF16, BF16, BF16], tm=tms, rope=rope, name="diff_proj_lat")
            n_ctx = cache_diff_k.shape[2]
            kctx = cache_diff_k[:, j].reshape(bs, n_ctx, nh).astype(BF16)
            vctx = cache_diff_v[:, j].reshape(bs, n_ctx, nh).astype(BF16)
            o = _diff_attn(q, k, v, kctx, vctx, lqk, gsub, tq=min(ls, 512), kc=_key_chunk(ls + n_ctx),
                           lam_init=lam_init)
            xs = _out_proj(o, w_o, xs, mod_s, tm=tms, name="diff_out_lat")
        else:
            gn = SSD_GROUPS * SSD_STATE
            nxbc = SSD_D_INNER + 2 * gn
            w_in = ssd_w_in[j]
            w = jnp.concatenate([w_in[:, :SSD_D_INNER + nxbc], _pad_cols(w_in[:, SSD_D_INNER + nxbc:], LANES)],
                                axis=1).astype(BF16)
            segs = [(SSD_D_INNER, "plain", 1.0), (SSD_D_INNER, "conv_silu", 1.0), (2 * gn, "conv_silu", 1.0),
                    (LANES, "plain", 1.0)]
            seg_dtypes = [BF16, F32, BF16, F32]
            cw = _conv_pack(ssd_conv_w[j], ssd_conv_b[j])
            zpad = jnp.zeros((LANES - 2 * SSD_HEADS,), F32)
            zh = jnp.zeros((SSD_HEADS,), F32)
            bias = jnp.concatenate([ssd_dt_bias_f[j], ssd_dt_bias_b[j], zpad]).reshape(1, LANES)
            a_f = jnp.concatenate([-jnp.exp(ssd_a_log_f[j]), zh, zpad]).reshape(1, LANES)
            a_b = jnp.concatenate([zh, -jnp.exp(ssd_a_log_b[j]), zpad]).reshape(1, LANES)
            dsk = jnp.concatenate([ssd_d[j], zh, zpad]).reshape(1, LANES)
            w_out = ssd_w_out[j].astype(BF16)
            rows = SSD_HEADS * SSD_HEAD_DIM
            for stream in ("p", "s"):
                if stream == "p":
                    x, mod, tm, bsz = xp, mod_p, tmp, bp
                    s0_f = jnp.zeros((bp, rows, SSD_STATE), F32)
                    s0_b = s0_f
                else:
                    x, mod, tm, bsz = xs, mod_s, tms, bs
                    s0_f = state_ssd_fwd[:, j].reshape(bs, rows, SSD_STATE)
                    s0_b = state_ssd_bwd[:, j].reshape(bs, rows, SSD_STATE)
                z, xc, bc, dt = _fused_proj(x, norm_mix[i], mod, w, segs, seg_dtypes, tm=tm, conv_w=cw,
                                            name="ssd_proj_" + stream)
                y_f, s_f = _ssd_scan(xc, bc, dt, bias, a_f, dsk, s0_f, reverse=False, add_skip=True)
                y_b, s_b = _ssd_scan(xc, bc, dt, bias, a_b, dsk, s0_b, reverse=True, add_skip=False)
                x = _ssd_out(y_f, y_b, z, ssd_norm[j], w_out, x, mod, tm=tm)
                if stream == "p":
                    xp = x
                    outs["ssd_f"] = s_f.reshape(bp, 1, SSD_HEADS, SSD_HEAD_DIM, SSD_STATE)
                    outs["ssd_b"] = s_b.reshape(bp, 1, SSD_HEADS, SSD_HEAD_DIM, SSD_STATE)
                else:
                    xs = x

        w_up = ffn_w_up[i].astype(BF16)
        cw = _conv_pack(ffn_conv_w[i], ffn_conv_b[i])
        wd = ffn_w_down[i].astype(BF16)
        last = i == DEPTH - 1
        xp = _ffn(xp, norm_ffn[i], mod_p, w_up, cw, wd, final_norm, tm=tmp, final_norm=last, name="ffn_p")
        xs = _ffn(xs, norm_ffn[i], mod_s, w_up, cw, wd, final_norm, tm=tms, final_norm=last, name="ffn_s")

    return (xp, xs, outs["win_k"], outs["win_v"], outs["gla_f"], outs["gla_b"],
            outs["diff_k"], outs["diff_v"], outs["ssd_f"], outs["ssd_b"])
```

```python
import functools
import math

import jax
import jax.numpy as jnp
from jax import lax
from jax.experimental import pallas as pl
from jax.experimental.pallas import tpu as pltpu

F32 = jnp.float32
BF16 = jnp.bfloat16

D_MODEL = 1024
DEPTH = 4
GRID_W = 64
EPS = 1e-6
ROPE_BASE = 10000.0

WA_HEADS = 16
WA_KV_HEADS = 4
WA_GROUP = 4
WA_HEAD_DIM = 64
WINDOW = 128

GLA_HEADS = 4
GLA_DK = 128
GLA_DV = 256
GLA_RANK = 16
GLA_TAU = 16.0
GLA_CHUNK = 64
GLA_CUM_ROWS = 256

DA_HEADS = 8
DA_HEAD_DIM = 64

SSD_D_INNER = 2048
SSD_HEAD_DIM = 64
SSD_HEADS = 32
SSD_GROUPS = 4
SSD_STATE = 128
SSD_CHUNK = 128

D_FF = 2816
FF_CHUNK = 256
FF_GROUP = 4

VMEM_LIMIT_BYTES = 56 * 1024 * 1024
SUBLANES = 8
LANES = 128
NEG_BIG = -1e30
LOG2_E = 1.4426950408889634
DIFF_STRIP = 16
SOFTMAX_STRIP = 32


def _cparams(sem):
    return pltpu.CompilerParams(dimension_semantics=sem, vmem_limit_bytes=VMEM_LIMIT_BYTES)


def _sigmoid(x):
    return 1.0 / (1.0 + jnp.exp(-x))


def _silu(x):
    return x * _sigmoid(x)


def _softplus(x):
    return jnp.maximum(x, 0.0) + jnp.log1p(jnp.exp(-jnp.abs(x)))


def _norm_mod(x, gamma, shift, scale):
    ms = jnp.mean(x * x, axis=-1, keepdims=True)
    y = (x * lax.rsqrt(ms + EPS)) * gamma
    return y * (1.0 + scale) + shift


def _split3(x):
    hi = x.astype(BF16)
    r1 = x - hi.astype(F32)
    mid = r1.astype(BF16)
    lo = (r1 - mid.astype(F32)).astype(BF16)
    return hi, mid, lo


def _exact_dot_left01(m01, x):
    hi, mid, lo = _split3(x)
    d = lambda p: jnp.dot(m01, p, preferred_element_type=F32)
    return d(hi) + d(mid) + d(lo)


def _dot_nt(a, b):
    return lax.dot_general(a, b, (((1,), (1,)), ((), ())), preferred_element_type=F32)


def _ada_kernel(c_ref, w_ref, b_ref, o_ref):
    h = _silu(c_ref[...]).astype(BF16)
    o_ref[0] = jnp.dot(h, w_ref[0].astype(BF16), preferred_element_type=F32) + b_ref[0]


def _ada_call(cond, ada_w, ada_b):
    rows = cond.shape[0]
    n = ada_w.shape[-1]
    tn = 1536
    return pl.pallas_call(
        _ada_kernel,
        grid=(DEPTH, n // tn),
        in_specs=[
            pl.BlockSpec((rows, D_MODEL), lambda l, j: (0, 0)),
            pl.BlockSpec((1, D_MODEL, tn), lambda l, j: (l, 0, j)),
            pl.BlockSpec((1, 1, tn), lambda l, j: (l, 0, j)),
        ],
        out_specs=pl.BlockSpec((1, rows, tn), lambda l, j: (l, 0, j)),
        out_shape=jax.ShapeDtypeStruct((DEPTH, rows, n), F32),
        compiler_params=_cparams(("parallel", "parallel")),
        name="ada_mod",
    )(cond, ada_w, ada_b.reshape(DEPTH, 1, n))


def _rope_apply(y, cos, sin):
    lane = lax.broadcasted_iota(jnp.int32, (1, LANES), 1)
    first = (lane % 32) < 16
    partner = jnp.where(first, pltpu.roll(y, LANES - 16, 1), pltpu.roll(y, 16, 1))
    return y * cos + partner * sin


def _fused_proj_kernel(*refs, tm, segs, has_conv, has_rope, n_tiles, chunk):
    it = iter(refs)
    x_ref = next(it)
    if has_conv:
        xp_ref = next(it)
        xn_ref = next(it)
    g_ref = next(it)
    mod_ref = next(it)
    w_ref = next(it)
    if has_rope:
        cos_ref = next(it)
        sin_ref = next(it)
    if has_conv:
        cw_ref = next(it)
    out_refs = [next(it) for _ in segs]
    h_ref = next(it)

    i = pl.program_id(1)
    gamma = g_ref[...]
    shift = mod_ref[0, 0:1, :]
    scale = mod_ref[0, 1:2, :]
    off = SUBLANES if has_conv else 0
    h_ref[off:off + tm, :] = _norm_mod(x_ref[0], gamma, shift, scale).astype(BF16)
    if has_conv:
        hp = _norm_mod(xp_ref[0], gamma, shift, scale)
        hn = _norm_mod(xn_ref[0], gamma, shift, scale)
        h_ref[0:SUBLANES, :] = jnp.where(i > 0, hp, 0.0).astype(BF16)
        h_ref[off + tm:off + tm + SUBLANES, :] = jnp.where(i < n_tiles - 1, hn, 0.0).astype(BF16)

    col = 0
    conv_col = 0
    for seg, o_ref in zip(segs, out_refs):
        width, epi, qscale = seg
        for c0 in range(0, width, chunk):
            wc = min(chunk, width - c0)
            w = w_ref[:, col + c0:col + c0 + wc]
            if epi == "conv_silu":
                u = jnp.dot(h_ref[...], w, preferred_element_type=F32)
                rows = tm + 2 * SUBLANES
                up = pltpu.roll(u, 1, 0)[off:off + tm]
                un = pltpu.roll(u, rows - 1, 0)[off:off + tm]
                uc = u[off:off + tm]
                cw = cw_ref[:, conv_col + c0:conv_col + c0 + wc]
                y = cw[0:1] * up + cw[1:2] * uc + cw[2:3] * un + cw[3:4]
                y = _silu(y)
            else:
                y = jnp.dot(h_ref[off:off + tm, :], w, preferred_element_type=F32)
                if qscale != 1.0:
                    y = y * qscale
                if epi == "rope":
                    cos = cos_ref[...]
                    sin = sin_ref[...]
                    y = jnp.concatenate(
                        [_rope_apply(y[:, k:k + LANES], cos, sin) for k in range(0, wc, LANES)], axis=1)
            o_ref[0, :, c0:c0 + wc] = y.astype(o_ref.dtype)
        col += width
        if epi == "conv_silu":
            conv_col += width


def _fused_proj(x, gamma, mod, w, segs, out_dtypes, *, tm, rope=None, conv_w=None, name):
    b, l, d = x.shape
    n_tiles = l // tm
    has_conv = conv_w is not None
    has_rope = rope is not None
    per_batch = mod.shape[0] > 1
    n_total = w.shape[1]
    bpt = tm // SUBLANES
    nblk8 = l // SUBLANES

    in_specs = [pl.BlockSpec((1, tm, d), lambda bi, i: (bi, i, 0))]
    args = [x]
    if has_conv:
        in_specs.append(pl.BlockSpec((1, SUBLANES, d), lambda bi, i: (bi, jnp.maximum(i * bpt - 1, 0), 0)))
        in_specs.append(pl.BlockSpec((1, SUBLANES, d), lambda bi, i: (bi, jnp.minimum((i + 1) * bpt, nblk8 - 1), 0)))
        args += [x, x]
    in_specs.append(pl.BlockSpec((1, d), lambda bi, i: (0, 0)))
    args.append(gamma.reshape(1, d))
    in_specs.append(pl.BlockSpec((1, 8, d), (lambda bi, i: (bi, 0, 0)) if per_batch else (lambda bi, i: (0, 0, 0))))
    args.append(mod)
    in_specs.append(pl.BlockSpec((d, n_total), lambda bi, i: (0, 0)))
    args.append(w)
    if has_rope:
        in_specs.append(pl.BlockSpec((tm, LANES), lambda bi, i: (i, 0)))
        in_specs.append(pl.BlockSpec((tm, LANES), lambda bi, i: (i, 0)))
        args += [rope[0], rope[1]]
    if has_conv:
        in_specs.append(pl.BlockSpec(conv_w.shape, lambda bi, i: (0, 0)))
        args.append(conv_w)

    out_specs = [pl.BlockSpec((1, tm, s[0]), lambda bi, i: (bi, i, 0)) for s in segs]
    out_shape = [jax.ShapeDtypeStruct((b, l, s[0]), dt) for s, dt in zip(segs, out_dtypes)]
    hrows = tm + (2 * SUBLANES if has_conv else 0)
    kern = functools.partial(_fused_proj_kernel, tm=tm, segs=tuple(segs), has_conv=has_conv,
                             has_rope=has_rope, n_tiles=n_tiles, chunk=512)
    return pl.pallas_call(
        kern,
        grid=(b, n_tiles),
        in_specs=in_specs,
        out_specs=out_specs,
        out_shape=out_shape,
        scratch_shapes=[pltpu.VMEM((hrows, d), BF16)],
        compiler_params=_cparams(("parallel", "parallel")),
        name=name,
    )(*args)


def _out_proj_kernel(o_ref, w_ref, x_ref, mod_ref, out_ref, *, gate_row):
    acc = jnp.dot(o_ref[0].astype(BF16), w_ref[...], preferred_element_type=F32)
    out_ref[0] = x_ref[0] + mod_ref[0, gate_row:gate_row + 1, :] * acc


def _mod_spec(mod):
    if mod.shape[0] > 1:
        return pl.BlockSpec((1, 8, D_MODEL), lambda bi, i: (bi, 0, 0))
    return pl.BlockSpec((1, 8, D_MODEL), lambda bi, i: (0, 0, 0))


def _out_proj(o, w, x, mod, *, tm, name):
    b, l, k = o.shape
    return pl.pallas_call(
        functools.partial(_out_proj_kernel, gate_row=2),
        grid=(b, l // tm),
        in_specs=[
            pl.BlockSpec((1, tm, k), lambda bi, i: (bi, i, 0)),
            pl.BlockSpec((k, D_MODEL), lambda bi, i: (0, 0)),
            pl.BlockSpec((1, tm, D_MODEL), lambda bi, i: (bi, i, 0)),
            _mod_spec(mod),
        ],
        out_specs=pl.BlockSpec((1, tm, D_MODEL), lambda bi, i: (bi, i, 0)),
        out_shape=jax.ShapeDtypeStruct((b, l, D_MODEL), F32),
        compiler_params=_cparams(("parallel", "parallel")),
        name=name,
    )(o, w, x, mod)


def _ffn_kernel(x_ref, xp_ref, xn_ref, g_ref, mod_ref, wu_ref, cw_ref, wd_ref, fg_ref,
                out_ref, h_ref, hnat_ref, acc_ref, act_ref, *, tm, n_tiles, final_norm):
    i = pl.program_id(1)
    gamma = g_ref[...]
    shift = mod_ref[0, 3:4, :]
    scale = mod_ref[0, 4:5, :]
    nv = tm // SUBLANES
    pitch = nv + SUBLANES
    rows = tm + 2 * SUBLANES
    n_slab = D_MODEL // LANES

    h_nat = _norm_mod(x_ref[0], gamma, shift, scale)
    for s in range(SUBLANES):
        for k in range(n_slab):
            hnat_ref[k, s * pitch:s * pitch + nv, :] = h_nat[s * nv:(s + 1) * nv, k * LANES:(k + 1) * LANES]

    def gather_rows(ref, start, stride):
        return jnp.concatenate([ref[k, pl.ds(start, SUBLANES, stride=stride), :] for k in range(n_slab)], axis=1)

    for v in range(0, nv, 2):
        pair = jnp.concatenate([gather_rows(hnat_ref, v, pitch), gather_rows(hnat_ref, v + 1, pitch)], axis=0)
        h_ref[v * SUBLANES:(v + 2) * SUBLANES, :] = pair.astype(BF16)
    hp = jnp.where(i > 0, _norm_mod(xp_ref[0], gamma, shift, scale), 0.0)
    hn = jnp.where(i < n_tiles - 1, _norm_mod(xn_ref[0], gamma, shift, scale), 0.0)
    h_ref[tm:rows, :] = jnp.concatenate([hp, hn], axis=0).astype(BF16)

    sub = lax.broadcasted_iota(jnp.int32, (SUBLANES, 1), 0)

    def conv(u, cw):
        main = u[0:tm]
        first_prev = pltpu.roll(jnp.where(sub == SUBLANES - 1, u[tm:tm + SUBLANES], u[tm - SUBLANES:tm]), 1, 0)
        last_next = pltpu.roll(jnp.where(sub == 0, u[tm + SUBLANES:rows], u[0:SUBLANES]), SUBLANES - 1, 0)
        prev = jnp.concatenate([first_prev, u[0:tm - SUBLANES]], axis=0)
        nxt = jnp.concatenate([u[SUBLANES:tm], last_next], axis=0)
        return cw[0:1] * prev + cw[1:2] * main + cw[2:3] * nxt + cw[3:4]

    n_chunks = D_FF // FF_CHUNK
    group_start = 0
    for c in range(n_chunks):
        gs = slice(c * FF_CHUNK, (c + 1) * FF_CHUNK)
        vs = slice(D_FF + c * FF_CHUNK, D_FF + (c + 1) * FF_CHUNK)
        h = h_ref[...]
        ug = jnp.dot(h, wu_ref[:, gs], preferred_element_type=F32)
        uv = jnp.dot(h, wu_ref[:, vs], preferred_element_type=F32)
        a = _silu(conv(ug, cw_ref[:, gs])) * conv(uv, cw_ref[:, vs])
        act_ref[:, gs] = a.astype(BF16)
        if (c + 1) % FF_GROUP == 0 or c == n_chunks - 1:
            ks = slice(group_start * FF_CHUNK, (c + 1) * FF_CHUNK)
            part = jnp.dot(act_ref[:, ks], wd_ref[ks, :], preferred_element_type=F32)
            for k in range(n_slab):
                if group_start == 0:
                    acc_ref[k] = part[:, k * LANES:(k + 1) * LANES]
                else:
                    acc_ref[k] += part[:, k * LANES:(k + 1) * LANES]
            group_start = c + 1
    gate = mod_ref[0, 5:6, :]
    for j in range(nv):
        s, v0 = divmod(j * SUBLANES, nv)
        rs = slice(j * SUBLANES, (j + 1) * SUBLANES)
        y = x_ref[0, rs, :] + gate * gather_rows(acc_ref, v0 * SUBLANES + s, SUBLANES)
        if final_norm:
            ms = jnp.mean(y * y, axis=-1, keepdims=True)
            y = (y * lax.rsqrt(ms + EPS)) * fg_ref[...]
        out_ref[0, rs, :] = y


def _ffn(x, gamma, mod, w_up, cw, w_down, final_gamma, *, tm, final_norm, name):
    b, l, d = x.shape
    n_tiles = l // tm
    bpt = tm // SUBLANES
    nblk8 = l // SUBLANES
    const2 = lambda bi, i: (0, 0)
    rows = tm + 2 * SUBLANES
    kern = functools.partial(_ffn_kernel, tm=tm, n_tiles=n_tiles, final_norm=final_norm)
    return pl.pallas_call(
        kern,
        grid=(b, n_tiles),
        in_specs=[
            pl.BlockSpec((1, tm, d), lambda bi, i: (bi, i, 0)),
            pl.BlockSpec((1, SUBLANES, d), lambda bi, i: (bi, jnp.maximum(i * bpt - 1, 0), 0)),
            pl.BlockSpec((1, SUBLANES, d), lambda bi, i: (bi, jnp.minimum((i + 1) * bpt, nblk8 - 1), 0)),
            pl.BlockSpec((1, d), const2),
            _mod_spec(mod),
            pl.BlockSpec(w_up.shape, const2, pipeline_mode=pl.Buffered(1)),
            pl.BlockSpec(cw.shape, const2),
            pl.BlockSpec(w_down.shape, const2, pipeline_mode=pl.Buffered(1)),
            pl.BlockSpec((1, d), const2),
        ],
        out_specs=pl.BlockSpec((1, tm, d), lambda bi, i: (bi, i, 0)),
        out_shape=jax.ShapeDtypeStruct((b, l, d), F32),
        scratch_shapes=[pltpu.VMEM((rows, d), BF16),
                        pltpu.VMEM((d // LANES, SUBLANES * (tm // SUBLANES + SUBLANES), LANES), F32),
                        pltpu.VMEM((d // LANES, tm, LANES), F32),
                        pltpu.VMEM((tm, D_FF), BF16)],
        compiler_params=_cparams(("parallel", "parallel")),
        name=name,
    )(x, x, x, gamma.reshape(1, d), mod, w_up, cw, w_down, final_gamma.reshape(1, d))


def _gqa_core(sink_ref, q_ref, kcat, vcat, bias, o_ref, tq):
    lane_head = lax.broadcasted_iota(jnp.int32, (1, WA_KV_HEADS * WA_HEAD_DIM), 1) // WA_HEAD_DIM
    width = WA_KV_HEADS * WA_HEAD_DIM

    def scores(g):
        qg = q_ref[0, :, g * width:(g + 1) * width]
        q4 = jnp.concatenate([jnp.where(lane_head == h, qg, jnp.zeros_like(qg)) for h in range(WA_KV_HEADS)], axis=0)
        return _dot_nt(q4, kcat)

    s_next = scores(0)
    for g in range(WA_GROUP):
        s = s_next
        if g + 1 < WA_GROUP:
            s_next = scores(g + 1)
        p_parts = []
        inv_parts = []
        for h in range(WA_KV_HEADS):
            sink = sink_ref[h * WA_GROUP + g]
            for t in range(0, tq, SOFTMAX_STRIP):
                st = s[h * tq + t:h * tq + t + SOFTMAX_STRIP]
                if bias is not None:
                    st = st + bias[t:t + SOFTMAX_STRIP]
                m = jnp.maximum(jnp.max(st, axis=-1, keepdims=True), sink)
                p = jnp.exp(st - m)
                inv_parts.append(1.0 / (jnp.sum(p, axis=-1, keepdims=True) + jnp.exp(sink - m)))
                p_parts.append(p.astype(BF16))
        p = jnp.concatenate(p_parts, axis=0)
        o4 = jnp.dot(p, vcat, preferred_element_type=F32) * jnp.concatenate(inv_parts, axis=0)
        og = jnp.zeros((tq, width), F32)
        for h in range(WA_KV_HEADS):
            og = og + jnp.where(lane_head == h, o4[h * tq:(h + 1) * tq], 0.0)
        o_ref[0, :, g * width:(g + 1) * width] = og.astype(o_ref.dtype)


def _win_ctx_kernel(sink_ref, q_ref, k_ref, v_ref, o_ref, *, tq):
    _gqa_core(sink_ref, q_ref, k_ref[0].astype(BF16), v_ref[0].astype(BF16), None, o_ref, tq)


def _win_ctx_attn(q, k, v, sink, *, tq):
    b, l, _ = q.shape
    kvw = WA_KV_HEADS * WA_HEAD_DIM
    return pl.pallas_call(
        functools.partial(_win_ctx_kernel, tq=tq),
        grid=(b, l // tq),
        in_specs=[
            pl.BlockSpec(memory_space=pltpu.SMEM),
            pl.BlockSpec((1, tq, D_MODEL), lambda bi, i: (bi, i, 0)),
            pl.BlockSpec((1, l, kvw), lambda bi, i: (bi, 0, 0)),
            pl.BlockSpec((1, l, kvw), lambda bi, i: (bi, 0, 0)),
        ],
        out_specs=pl.BlockSpec((1, tq, D_MODEL), lambda bi, i: (bi, i, 0)),
        out_shape=jax.ShapeDtypeStruct((b, l, D_MODEL), BF16),
        compiler_params=_cparams(("parallel", "parallel")),
        name="win_ctx_attn",
    )(sink, q, k, v)


def _win_lat_kernel(sink_ref, q_ref, kp_ref, kc_ref, kn_ref, vp_ref, vc_ref, vn_ref, kx_ref, vx_ref, o_ref,
                    kcat, vcat, *, tq, seq_len, n_ctx):
    i = pl.program_id(1)
    halo = WINDOW
    n_lat = tq + 2 * halo
    nk = n_lat + n_ctx
    for src, dst in ((kp_ref, kcat), (vp_ref, vcat)):
        dst[0:halo] = src[0]
    for src, dst in ((kc_ref, kcat), (vc_ref, vcat)):
        dst[halo:halo + tq] = src[0]
    for src, dst in ((kn_ref, kcat), (vn_ref, vcat)):
        dst[halo + tq:n_lat] = src[0]
    for src, dst in ((kx_ref, kcat), (vx_ref, vcat)):
        dst[n_lat:nk] = src[0]
    r = lax.broadcasted_iota(jnp.int32, (tq, nk), 0)
    c = lax.broadcasted_iota(jnp.int32, (tq, nk), 1)
    kpos = i * tq - halo + c
    ok = (c >= n_lat) | ((kpos >= 0) & (kpos < seq_len) & (jnp.abs(r + halo - c) <= WINDOW))
    bias = jnp.where(ok, 0.0, NEG_BIG).astype(F32)
    _gqa_core(sink_ref, q_ref, kcat[...], vcat[...], bias, o_ref, tq)


def _win_lat_attn(q, k, v, kctx, vctx, sink, *, tq):
    b, l, _ = q.shape
    halo = WINDOW
    assert tq % halo == 0 and l % tq == 0
    nq = l // tq
    per = tq // halo
    n_halo_blocks = l // halo
    n_ctx = kctx.shape[1]
    kvw = WA_KV_HEADS * WA_HEAD_DIM
    n_lat = tq + 2 * halo
    prev = lambda bi, i: (bi, jnp.maximum(i * per - 1, 0), 0)
    cur = lambda bi, i: (bi, i, 0)
    nxt = lambda bi, i: (bi, jnp.minimum((i + 1) * per, n_halo_blocks - 1), 0)
    halo_spec = lambda f: pl.BlockSpec((1, halo, kvw), f)
    cur_spec = pl.BlockSpec((1, tq, kvw), cur)
    return pl.pallas_call(
        functools.partial(_win_lat_kernel, tq=tq, seq_len=l, n_ctx=n_ctx),
        grid=(b, nq),
        in_specs=[
            pl.BlockSpec(memory_space=pltpu.SMEM),
            pl.BlockSpec((1, tq, D_MODEL), cur),
            halo_spec(prev), cur_spec, halo_spec(nxt),
            halo_spec(prev), cur_spec, halo_spec(nxt),
            pl.BlockSpec((1, n_ctx, kvw), lambda bi, i: (bi, 0, 0)),
            pl.BlockSpec((1, n_ctx, kvw), lambda bi, i: (bi, 0, 0)),
        ],
        out_specs=pl.BlockSpec((1, tq, D_MODEL), cur),
        out_shape=jax.ShapeDtypeStruct((b, l, D_MODEL), BF16),
        scratch_shapes=[pltpu.VMEM((n_lat + n_ctx, kvw), BF16), pltpu.VMEM((n_lat + n_ctx, kvw), BF16)],
        compiler_params=_cparams(("parallel", "parallel")),
        name="win_lat_attn",
    )(sink, q, k, k, k, v, v, v, kctx, vctx)


def _diff_kernel(lqk_ref, gsub_ref, q_ref, k_ref, v_ref, *rest, tq, kc, n_keys, has_ctx, lam_init):
    if has_ctx:
        kx_ref, vx_ref, o_ref, kall, vall, m_ref, acc_ref, s_a, s_b, p_ref, alpha_ref = rest
    else:
        o_ref, kall, vall, m_ref, acc_ref, s_a, s_b, p_ref, alpha_ref = rest
    hd = DA_HEAD_DIM
    lam = (jnp.exp(jnp.sum(lqk_ref[0:1, :] * lqk_ref[1:2, :], axis=-1, keepdims=True))
           - jnp.exp(jnp.sum(lqk_ref[2:3, :] * lqk_ref[3:4, :], axis=-1, keepdims=True)) + lam_init)
    hw = 2 * hd

    @pl.when(pl.program_id(2) == 0)
    def _():
        lat = k_ref.shape[1]
        kall[0:lat, :] = k_ref[0].astype(BF16)
        vall[0:lat, 0:hw] = v_ref[0].astype(BF16)
        if has_ctx:
            kall[lat:n_keys, :] = kx_ref[0]
            vall[lat:n_keys, 0:hw] = vx_ref[0]
        vall[:, hw:2 * hw] = jnp.ones((n_keys, hw), BF16)

    q = q_ref[0]
    lane = lax.broadcasted_iota(jnp.int32, (1, hw), 1)
    zero = jnp.zeros_like(q)
    q2 = jnp.concatenate([jnp.where(lane < hd, q, zero), jnp.where(lane >= hd, q, zero)], axis=0)

    n_chunks = n_keys // kc
    s_bufs = (s_a, s_b)
    s_bufs[0][...] = _dot_nt(q2, kall[0:kc, :])
    for j in range(n_chunks):
        if j + 1 < n_chunks:
            s_bufs[(j + 1) % 2][...] = _dot_nt(q2, kall[(j + 1) * kc:(j + 2) * kc, :])
        s_ref = s_bufs[j % 2]
        for t in range(0, 2 * tq, DIFF_STRIP):
            rows = slice(t, t + DIFF_STRIP)
            st = s_ref[rows, :]
            m_chunk = jnp.max(st, axis=-1, keepdims=True)
            if j == 0:
                m_new = jnp.broadcast_to(m_chunk, (DIFF_STRIP, LANES))
            else:
                m_old = m_ref[rows, :]
                m_new = jnp.maximum(m_old, m_chunk)
                alpha_ref[rows, :] = jnp.exp(m_old - m_new)
            p_ref[rows, :] = jnp.exp(st - jnp.concatenate([m_new] * (kc // LANES), axis=1)).astype(BF16)
            if j + 1 < n_chunks:
                m_ref[rows, :] = m_new
        pv = jnp.dot(p_ref[...], vall[j * kc:(j + 1) * kc, :], preferred_element_type=F32)
        if j == 0:
            acc_ref[...] = pv
        else:
            alpha = alpha_ref[...]
            acc_ref[...] = jnp.concatenate([alpha, alpha], axis=1) * acc_ref[...] + pv
    acc = acc_ref[...]
    o2 = acc[:, 0:hw] / acc[:, hw:2 * hw]
    o = o2[0:tq] - lam * o2[tq:2 * tq]
    ms = jnp.mean(o * o, axis=-1, keepdims=True)
    o = (o * lax.rsqrt(ms + EPS)) * gsub_ref[...] * (1.0 - lam_init)
    o_ref[0] = o.astype(o_ref.dtype)


def _diff_attn(q, k, v, kctx, vctx, lqk, gsub, *, tq, kc, lam_init):
    b, l, _ = q.shape
    hw = 2 * DA_HEAD_DIM
    has_ctx = kctx is not None
    in_specs = [
        pl.BlockSpec((8, hw), lambda bi, h, i: (0, 0)),
        pl.BlockSpec((1, hw), lambda bi, h, i: (0, 0)),
        pl.BlockSpec((1, tq, hw), lambda bi, h, i: (bi, i, h)),
        pl.BlockSpec((1, l, hw), lambda bi, h, i: (bi, 0, h)),
        pl.BlockSpec((1, l, hw), lambda bi, h, i: (bi, 0, h)),
    ]
    args = [lqk, gsub, q, k, v]
    n_keys = l
    if has_ctx:
        n_ctx = kctx.shape[1]
        n_keys = l + n_ctx
        in_specs.append(pl.BlockSpec((1, n_ctx, hw), lambda bi, h, i: (bi, 0, h)))
        in_specs.append(pl.BlockSpec((1, n_ctx, hw), lambda bi, h, i: (bi, 0, h)))
        args += [kctx, vctx]
    assert n_keys % kc == 0 and kc % LANES == 0
    kern = functools.partial(_diff_kernel, tq=tq, kc=kc, n_keys=n_keys, has_ctx=has_ctx, lam_init=lam_init)
    return pl.pallas_call(
        kern,
        grid=(b, DA_HEADS, l // tq),
        in_specs=in_specs,
        out_specs=pl.BlockSpec((1, tq, hw), lambda bi, h, i: (bi, i, h)),
        out_shape=jax.ShapeDtypeStruct((b, l, D_MODEL), BF16),
        scratch_shapes=[pltpu.VMEM((n_keys, hw), BF16), pltpu.VMEM((n_keys, 2 * hw), BF16),
                        pltpu.VMEM((2 * tq, LANES), F32), pltpu.VMEM((2 * tq, 2 * hw), F32),
                        pltpu.VMEM((2 * tq, kc), F32), pltpu.VMEM((2 * tq, kc), F32),
                        pltpu.VMEM((2 * tq, kc), BF16), pltpu.VMEM((2 * tq, LANES), F32)],
        compiler_params=_cparams(("parallel", "parallel", "arbitrary")),
        name="diff_lat_attn" if has_ctx else "diff_ctx_attn",
    )(*args)


def _gla_scan_kernel(qf_ref, kf_ref, vf_ref, zf_ref, qb_ref, kb_ref, vb_ref, zb_ref, w2_ref, b2_ref, s0_ref,
                     of_ref, ob_ref, sf_ref, st_scr, qd_scr, kd_scr, kr_scr, vt_scr, mt_scr, el_scr,
                     *, blk, n_blocks):
    i = pl.program_id(1)
    ch = GLA_CHUNK
    nk = GLA_HEADS * GLA_DK
    n_sub = blk // ch
    dirs = ((qf_ref, kf_ref, vf_ref, zf_ref, of_ref), (qb_ref, kb_ref, vb_ref, zb_ref, ob_ref))

    @pl.when(i == 0)
    def _():
        for d in range(2):
            for h in range(GLA_HEADS):
                st_scr[d, h] = jnp.transpose(s0_ref[d, 0, h])

    cs = min(blk, GLA_CUM_ROWS)
    r = lax.broadcasted_iota(jnp.int32, (cs, cs), 0)
    c = lax.broadcasted_iota(jnp.int32, (cs, cs), 1)
    same_chunk = (r // ch) == (c // ch)
    rr = lax.broadcasted_iota(jnp.int32, (ch, ch), 0)
    cc = lax.broadcasted_iota(jnp.int32, (ch, ch), 1)
    keeps = (rr >= cc, rr <= cc)
    edges = (ch - 1, 0)

    for d, (q_ref, k_ref, v_ref, zl_ref, _) in enumerate(dirs):
        tri = (same_chunk & ((r <= c) if d == 1 else (r >= c))).astype(BF16)
        z = jnp.dot(zl_ref[0].astype(BF16), w2_ref[d], preferred_element_type=F32) + b2_ref[d]
        logg = (jnp.minimum(z, 0.0) - jnp.log1p(jnp.exp(-jnp.abs(z)))) / GLA_TAU
        cum = jnp.concatenate([_exact_dot_left01(tri, logg[t:t + cs]) for t in range(0, blk, cs)], axis=0)
        lasts = [cum[s * ch + edges[d]:s * ch + edges[d] + 1, :] for s in range(n_sub)]
        blast = jnp.concatenate([jnp.broadcast_to(b, (ch, nk)) for b in lasts], axis=0)
        k = k_ref[0]
        qd_scr[d] = ((q_ref[0] * (GLA_DK ** -0.5)) * jnp.exp(cum)).astype(BF16)
        kd_scr[d] = (k * jnp.exp(-cum)).astype(BF16)
        kr_scr[d] = (k * jnp.exp(blast - cum)).astype(BF16)
        for s in range(n_sub):
            el_scr[d, s:s + 1, :] = jnp.exp(lasts[s])
        vt_scr[d] = jnp.transpose(v_ref[0].astype(F32))

    lane_chunk = lax.broadcasted_iota(jnp.int32, (1, 2 * ch), 1) // ch
    for s in range(n_sub):
        rows = slice(s * ch, (s + 1) * ch)
        pair = slice((s // 2) * 2 * ch, (s // 2 + 1) * 2 * ch)
        for h in range(GLA_HEADS):
            ks = slice(h * GLA_DK, (h + 1) * GLA_DK)
            vs = slice(h * GLA_DV, (h + 1) * GLA_DV)
            for d, (_, _, v_ref, _, o_ref) in enumerate(dirs):
                att = jnp.where(keeps[d], _dot_nt(qd_scr[d, rows, ks], kd_scr[d, rows, ks]), 0.0)
                o_ref[0, rows, vs] = jnp.dot(att.astype(BF16), v_ref[0, rows, vs].astype(BF16),
                                             preferred_element_type=F32)
                vt = jnp.where(lane_chunk == s % 2, vt_scr[d, vs, pair], 0.0).astype(BF16)
                mt_scr[d, s, h] = jnp.dot(vt, kr_scr[d, pair, ks], preferred_element_type=F32)

    for step in range(n_sub):
        for h in range(GLA_HEADS):
            ks = slice(h * GLA_DK, (h + 1) * GLA_DK)
            vs = slice(h * GLA_DV, (h + 1) * GLA_DV)
            for d, (_, _, _, _, o_ref) in enumerate(dirs):
                s = step if d == 0 else n_sub - 1 - step
                rows = slice(s * ch, (s + 1) * ch)
                st = st_scr[d, h]
                o_ref[0, rows, vs] += _dot_nt(qd_scr[d, rows, ks], st.astype(BF16))
                st_scr[d, h] = st * el_scr[d, s:s + 1, ks] + mt_scr[d, s, h]

    @pl.when(i == n_blocks - 1)
    def _():
        for d in range(2):
            for h in range(GLA_HEADS):
                sf_ref[d, 0, h] = jnp.transpose(st_scr[d, h])


def _gla_scan(q, k, v, zl, w2, b2, s0, *, blk):
    b, l, nk = q.shape
    nv = v.shape[-1]
    n_blocks = l // blk
    n_sub = blk // GLA_CHUNK
    assert blk % (2 * GLA_CHUNK) == 0 and n_sub <= SUBLANES
    fwd = lambda bi, i: (bi, i, 0)
    bwd = lambda bi, i: (bi, n_blocks - 1 - i, 0)
    row_specs = lambda f: [pl.BlockSpec((1, blk, nk), f), pl.BlockSpec((1, blk, nk), f),
                           pl.BlockSpec((1, blk, nv), f), pl.BlockSpec((1, blk, LANES), f)]
    state_spec = pl.BlockSpec((2, 1, GLA_HEADS, GLA_DK, GLA_DV), lambda bi, i: (0, bi, 0, 0, 0))
    return pl.pallas_call(
        functools.partial(_gla_scan_kernel, blk=blk, n_blocks=n_blocks),
        grid=(b, n_blocks),
        in_specs=row_specs(fwd) + row_specs(bwd) + [
            pl.BlockSpec((2, LANES, nk), lambda bi, i: (0, 0, 0)),
            pl.BlockSpec((2, 1, nk), lambda bi, i: (0, 0, 0)),
            state_spec,
        ],
        out_specs=[pl.BlockSpec((1, blk, nv), fwd), pl.BlockSpec((1, blk, nv), bwd), state_spec],
        out_shape=[jax.ShapeDtypeStruct((b, l, nv), F32), jax.ShapeDtypeStruct((b, l, nv), F32),
                   jax.ShapeDtypeStruct((2, b, GLA_HEADS, GLA_DK, GLA_DV), F32)],
        scratch_shapes=[pltpu.VMEM((2, GLA_HEADS, GLA_DV, GLA_DK), F32),
                        pltpu.VMEM((2, blk, nk), BF16), pltpu.VMEM((2, blk, nk), BF16),
                        pltpu.VMEM((2, blk, nk), BF16),
                        pltpu.VMEM((2, nv, blk), F32),
                        pltpu.VMEM((2, n_sub, GLA_HEADS, GLA_DV, GLA_DK), F32),
                        pltpu.VMEM((2, SUBLANES, nk), F32)],
        compiler_params=_cparams(("parallel", "arbitrary")),
        name="gla_scan",
    )(q, k, v, zl, q, k, v, zl, w2, b2, s0)


def _gla_out_kernel(of_ref, ob_ref, r_ref, gh_ref, w_ref, x_ref, mod_ref, out_ref):
    o = of_ref[0] + ob_ref[0]
    parts = []
    for h in range(GLA_HEADS):
        oh = o[:, h * GLA_DV:(h + 1) * GLA_DV]
        ms = jnp.mean(oh * oh, axis=-1, keepdims=True)
        parts.append((oh * lax.rsqrt(ms + EPS)) * gh_ref[...])
    y = jnp.concatenate(parts, axis=1) * _silu(r_ref[0].astype(F32))
    acc = jnp.dot(y.astype(BF16), w_ref[...], preferred_element_type=F32)
    out_ref[0] = x_ref[0] + mod_ref[0, 2:3, :] * acc


def _gla_out(o_f, o_b, r, g_head, w, x, mod, *, tm):
    b, l, nv = o_f.shape
    row = lambda bi, i: (bi, i, 0)
    return pl.pallas_call(
        _gla_out_kernel,
        grid=(b, l // tm),
        in_specs=[
            pl.BlockSpec((1, tm, nv), row),
            pl.BlockSpec((1, tm, nv), row),
            pl.BlockSpec((1, tm, nv), row),
            pl.BlockSpec((1, GLA_DV), lambda bi, i: (0, 0)),
            pl.BlockSpec((nv, D_MODEL), lambda bi, i: (0, 0)),
            pl.BlockSpec((1, tm, D_MODEL), row),
            _mod_spec(mod),
        ],
        out_specs=pl.BlockSpec((1, tm, D_MODEL), row),
        out_shape=jax.ShapeDtypeStruct((b, l, D_MODEL), F32),
        compiler_params=_cparams(("parallel", "parallel")),
        name="gla_out",
    )(o_f, o_b, r, g_head.reshape(1, GLA_DV), w, x, mod)


def _ssd_scan_kernel(x_ref, b_ref, c_ref, dt_ref, bias_ref, a_ref, dsk_ref, s0_ref, y_ref, sf_ref, s_scr, xt_scr,
                     *, reverse, n_chunks, lane_off, add_skip):
    i = pl.program_id(1)
    ch = SSD_CHUNK
    p = SSD_HEAD_DIM
    hpg = SSD_HEADS // SSD_GROUPS

    @pl.when(i == 0)
    def _():
        s_scr[...] = s0_ref[0]

    r = lax.broadcasted_iota(jnp.int32, (ch, ch), 0)
    c = lax.broadcasted_iota(jnp.int32, (ch, ch), 1)
    keep = (r <= c) if reverse else (r >= c)
    tri = keep.astype(BF16)
    edge = 0 if reverse else ch - 1

    dt = _softplus(dt_ref[0] + bias_ref[...])
    la = dt * a_ref[...]
    cum = _exact_dot_left01(tri, la) * LOG2_E
    cum_t = jnp.transpose(cum)
    dt_t = jnp.transpose(dt)
    to_end_t = jnp.transpose(jnp.exp2(cum[edge:edge + 1, :] - cum) * dt)
    xt_scr[...] = jnp.transpose(x_ref[0])
    keep_f = keep.astype(F32)
    lane = lax.broadcasted_iota(jnp.int32, (1, LANES), 1)
    lo_half = lane < p

    for g in range(SSD_GROUPS):
        bg = b_ref[0, :, g * SSD_STATE:(g + 1) * SSD_STATE].astype(BF16)
        cg = c_ref[0, :, g * SSD_STATE:(g + 1) * SSD_STATE].astype(BF16)
        cb = _dot_nt(cg, bg) * keep_f
        gs = slice(g * hpg * p, (g + 1) * hpg * p)
        s_g = s_scr[gs, :]
        y_inter = _dot_nt(cg, s_g.astype(BF16))
        for pair in range(hpg // 2):
            cols = slice((g * hpg + 2 * pair) * p, (g * hpg + 2 * pair + 2) * p)
            x_pair = x_ref[0, :, cols]
            ws = []
            e_is = []
            for t in range(2):
                hl = lane_off + g * hpg + 2 * pair + t
                cum_i = jnp.broadcast_to(cum[:, hl:hl + 1], (ch, ch))
                seg = jnp.minimum(cum_i - cum_t[hl:hl + 1, :], 0.0)
                ws.append((cb * jnp.exp2(seg) * dt_t[hl:hl + 1, :]).astype(BF16))
                e_is.append(jnp.exp2(cum_i))
            xb = x_pair.astype(BF16)
            zero = jnp.zeros_like(xb)
            x_bd = jnp.concatenate([jnp.where(lo_half, xb, zero), jnp.where(lo_half, zero, xb)], axis=0)
            y_pair = jnp.dot(jnp.concatenate(ws, axis=1), x_bd, preferred_element_type=F32)
            y_pair = y_pair + y_inter[:, 2 * pair * p:(2 * pair + 2) * p] * jnp.where(lo_half, e_is[0], e_is[1])
            if add_skip:
                hl0 = g * hpg + 2 * pair
                dsk = jnp.where(lo_half, dsk_ref[:, hl0:hl0 + 1], dsk_ref[:, hl0 + 1:hl0 + 2])
                y_pair = y_pair + x_pair * dsk
            y_ref[0, :, cols] = y_pair.astype(y_ref.dtype)
        xs_t = []
        for hh in range(hpg):
            hl = lane_off + g * hpg + hh
            rs = slice((g * hpg + hh) * p, (g * hpg + hh + 1) * p)
            xs_t.append((xt_scr[rs, :] * to_end_t[hl:hl + 1, :]).astype(BF16))
        ds = jnp.dot(jnp.concatenate(xs_t, axis=0), bg, preferred_element_type=F32)
        for hh in range(hpg):
            hl = lane_off + g * hpg + hh
            tot = jnp.exp2(cum_t[hl:hl + 1, edge:edge + 1])
            rs = slice((g * hpg + hh) * p, (g * hpg + hh + 1) * p)
            s_scr[rs, :] = s_scr[rs, :] * tot + ds[hh * p:(hh + 1) * p, :]

    @pl.when(i == n_chunks - 1)
    def _():
        sf_ref[0] = s_scr[...]


def _ssd_scan(x, bc, dt, bias, a, dskip, s0, *, reverse, add_skip):
    b, l, _ = x.shape
    ch = SSD_CHUNK
    n_chunks = l // ch
    gn = SSD_GROUPS * SSD_STATE
    rows = SSD_HEADS * SSD_HEAD_DIM

    def at(col):
        if reverse:
            return lambda bi, i: (bi, n_chunks - 1 - i, col)
        return lambda bi, i: (bi, i, col)

    vec = pl.BlockSpec((1, LANES), lambda bi, i: (0, 0))
    state_spec = pl.BlockSpec((1, rows, SSD_STATE), lambda bi, i: (bi, 0, 0))
    kern = functools.partial(_ssd_scan_kernel, reverse=reverse, n_chunks=n_chunks,
                             lane_off=SSD_HEADS if reverse else 0, add_skip=add_skip)
    return pl.pallas_call(
        kern,
        grid=(b, n_chunks),
        in_specs=[
            pl.BlockSpec((1, ch, SSD_D_INNER), at(0)),
            pl.BlockSpec((1, ch, gn), at(0)),
            pl.BlockSpec((1, ch, gn), at(1)),
            pl.BlockSpec((1, ch, LANES), at(0)),
            vec, vec, vec,
            state_spec,
        ],
        out_specs=[pl.BlockSpec((1, ch, SSD_D_INNER), at(0)), state_spec],
        out_shape=[jax.ShapeDtypeStruct((b, l, SSD_D_INNER), BF16),
                   jax.ShapeDtypeStruct((b, rows, SSD_STATE), F32)],
        scratch_shapes=[pltpu.VMEM((rows, SSD_STATE), F32), pltpu.VMEM((SSD_D_INNER, ch), F32)],
        compiler_params=_cparams(("parallel", "arbitrary")),
        name="ssd_scan_bwd" if reverse else "ssd_scan_fwd",
    )(x, bc, bc, dt, bias, a, dskip, s0)


def _ssd_out_kernel(yf_ref, yb_ref, z_ref, gn_ref, w_ref, x_ref, mod_ref, out_ref):
    y = (yf_ref[0].astype(F32) + yb_ref[0].astype(F32)) * _silu(z_ref[0].astype(F32))
    ms = jnp.mean(y * y, axis=-1, keepdims=True)
    y = (y * lax.rsqrt(ms + EPS)) * gn_ref[...]
    acc = jnp.dot(y.astype(BF16), w_ref[...], preferred_element_type=F32)
    out_ref[0] = x_ref[0] + mod_ref[0, 2:3, :] * acc


def _ssd_out(y_f, y_b, z, g_norm, w, x, mod, *, tm):
    b, l, di = y_f.shape
    row = lambda bi, i: (bi, i, 0)
    return pl.pallas_call(
        _ssd_out_kernel,
        grid=(b, l // tm),
        in_specs=[
            pl.BlockSpec((1, tm, di), row),
            pl.BlockSpec((1, tm, di), row),
            pl.BlockSpec((1, tm, di), row),
            pl.BlockSpec((1, di), lambda bi, i: (0, 0)),
            pl.BlockSpec((di, D_MODEL), lambda bi, i: (0, 0)),
            pl.BlockSpec((1, tm, D_MODEL), row),
            _mod_spec(mod),
        ],
        out_specs=pl.BlockSpec((1, tm, D_MODEL), row),
        out_shape=jax.ShapeDtypeStruct((b, l, D_MODEL), F32),
        compiler_params=_cparams(("parallel", "parallel")),
        name="ssd_out",
    )(y_f, y_b, z, g_norm.reshape(1, di), w, x, mod)


def _rope_tables(n_tokens, dim):
    rows = n_tokens // GRID_W
    row = jnp.repeat(jnp.arange(rows, dtype=F32), GRID_W)
    col = jnp.tile(jnp.arange(GRID_W, dtype=F32), rows)
    axis_dim = dim // 2
    inv = ROPE_BASE ** (-jnp.arange(0, axis_dim, 2, dtype=F32) / axis_dim)
    ar = row[:, None] * inv
    ac = col[:, None] * inv
    cos = jnp.concatenate([jnp.cos(ar), jnp.cos(ar), jnp.cos(ac), jnp.cos(ac)], axis=1)
    sin = jnp.concatenate([-jnp.sin(ar), jnp.sin(ar), -jnp.sin(ac), jnp.sin(ac)], axis=1)
    reps = LANES // dim
    return jnp.tile(cos, (1, reps)), jnp.tile(sin, (1, reps))


def _pad_cols(w, width):
    return jnp.pad(w, ((0, 0), (0, width - w.shape[1])))


def _conv_pack(conv_w, conv_b):
    return jnp.concatenate([conv_w, conv_b[None], jnp.zeros((4, conv_w.shape[1]), F32)], axis=0)


def _tm_for(l):
    return min(l, 512)


def _key_chunk(n_keys):
    for kc in (2304, 1536, 768, 512, 256):
        if n_keys % kc == 0:
            return kc
    return n_keys


def kernel(x_prompt, x_sample, c, c_ctx, cache_win_k, cache_win_v, state_gla_fwd, state_gla_bwd, cache_diff_k, cache_diff_v, state_ssd_fwd, state_ssd_bwd, ada_w, ada_b, norm_mix, norm_ffn, ffn_w_up, ffn_conv_w, ffn_conv_b, ffn_w_down, final_norm, win_w_qkv, win_w_o, win_sink, gla_w_qkvr, gla_w_gf1, gla_w_gf2, gla_b_gf, gla_w_gb1, gla_w_gb2, gla_b_gb, gla_norm, gla_w_o, diff_w_qkv, diff_lq1, diff_lk1, diff_lq2, diff_lk2, diff_norm, diff_w_o, ssd_w_in, ssd_conv_w, ssd_conv_b, ssd_a_log_f, ssd_a_log_b, ssd_dt_bias_f, ssd_dt_bias_b, ssd_d, ssd_norm, ssd_w_out):
    xp, xs = x_prompt, x_sample
    bp, lp, d = xp.shape
    bs, ls, _ = xs.shape
    tmp, tms = _tm_for(lp), _tm_for(ls)

    n_cond = 1 + bs
    cond_rows = -(-n_cond // SUBLANES) * SUBLANES
    cond = jnp.concatenate([c_ctx[None], c, jnp.zeros((cond_rows - n_cond, d), F32)], axis=0)
    mods = _ada_call(cond, ada_w, ada_b).reshape(DEPTH, cond_rows, 6, d)
    mods = jnp.pad(mods, ((0, 0), (0, 0), (0, 2), (0, 0)))

    outs = {}
    for i in range(DEPTH):
        kind, j = i % 4, i // 4
        mod_p = mods[i, 0:1]
        mod_s = mods[i, 1:1 + bs]
        if kind == 0:
            nq = WA_HEADS * WA_HEAD_DIM
            nkv = WA_KV_HEADS * WA_HEAD_DIM
            perm = jnp.arange(nq).reshape(WA_KV_HEADS, WA_GROUP, WA_HEAD_DIM).transpose(1, 0, 2).reshape(-1)
            wq = win_w_qkv[j][:, :nq][:, perm]
            w = jnp.concatenate([wq, win_w_qkv[j][:, nq:]], axis=1).astype(BF16)
            w_o = win_w_o[j][perm, :].astype(BF16)
            sink = win_sink[j]
            qscale = WA_HEAD_DIM ** -0.5
            q, k, v = _fused_proj(xp, norm_mix[i], mod_p, w,
                                  [(nq, "plain", qscale), (nkv, "plain", 1.0), (nkv, "plain", 1.0)],
                                  [BF16, F32, F32], tm=tmp, name="win_proj_ctx")
            outs["win_k"] = k.reshape(bp, 1, lp, WA_KV_HEADS, WA_HEAD_DIM)
            outs["win_v"] = v.reshape(bp, 1, lp, WA_KV_HEADS, WA_HEAD_DIM)
            o = _win_ctx_attn(q, k, v, sink, tq=min(lp, 128))
            xp = _out_proj(o, w_o, xp, mod_p, tm=tmp, name="win_out_ctx")
            rope = _rope_tables(ls, WA_HEAD_DIM)
            q, k, v = _fused_proj(xs, norm_mix[i], mod_s, w,
                                  [(nq, "rope", qscale), (nkv, "rope", 1.0), (nkv, "plain", 1.0)],
                                  [BF16, BF16, BF16], tm=tms, rope=rope, name="win_proj_lat")
            n_ctx = cache_win_k.shape[2]
            kctx = cache_win_k[:, j].reshape(bs, n_ctx, nkv).astype(BF16)
            vctx = cache_win_v[:, j].reshape(bs, n_ctx, nkv).astype(BF16)
            o = _win_lat_attn(q, k, v, kctx, vctx, sink, tq=min(ls, WINDOW))
            xs = _out_proj(o, w_o, xs, mod_s, tm=tms, name="win_out_lat")
        elif kind == 1:
            nk = GLA_HEADS * GLA_DK
            nv = GLA_HEADS * GLA_DV
            w1 = _pad_cols(jnp.concatenate([gla_w_gf1[j], gla_w_gb1[j]], axis=1), LANES)
            w = jnp.concatenate([gla_w_qkvr[j], w1], axis=1).astype(BF16)
            segs = [(nk, "plain", 1.0), (nk, "plain", 1.0), (nv, "plain", 1.0), (nv, "plain", 1.0),
                    (LANES, "plain", 1.0)]
            zrows = jnp.zeros((LANES - 2 * GLA_RANK, nk), F32)
            w2_f = jnp.concatenate([gla_w_gf2[j], jnp.zeros((GLA_RANK, nk), F32), zrows], axis=0).astype(BF16)
            w2_b = jnp.concatenate([jnp.zeros((GLA_RANK, nk), F32), gla_w_gb2[j], zrows], axis=0).astype(BF16)
            w2 = jnp.stack([w2_f, w2_b], axis=0)
            b2 = jnp.stack([gla_b_gf[j], gla_b_gb[j]], axis=0).reshape(2, 1, nk)
            w_o = gla_w_o[j].astype(BF16)
            dtypes = [F32, F32, BF16, BF16, F32]
            for stream in ("p", "s"):
                if stream == "p":
                    x, mod, tm = xp, mod_p, tmp
                    s0 = jnp.zeros((2, bp, GLA_HEADS, GLA_DK, GLA_DV), F32)
                else:
                    x, mod, tm = xs, mod_s, tms
                    s0 = jnp.stack([state_gla_fwd[:, j], state_gla_bwd[:, j]], axis=0)
                q, k, v, r, zl = _fused_proj(x, norm_mix[i], mod, w, segs, dtypes, tm=tm,
                                             name="gla_proj_" + stream)
                o_f, o_b, s_fb = _gla_scan(q, k, v, zl, w2, b2, s0, blk=min(x.shape[1], 512))
                x = _gla_out(o_f, o_b, r, gla_norm[j], w_o, x, mod, tm=tm)
                if stream == "p":
                    xp = x
                    outs["gla_f"] = s_fb[0][:, None]
                    outs["gla_b"] = s_fb[1][:, None]
                else:
                    xs = x
        elif kind == 2:
            lam_init = 0.8 - 0.6 * math.exp(-0.3 * i)
            nh = DA_HEADS * 2 * DA_HEAD_DIM
            w = diff_w_qkv[j].astype(BF16)
            w_o = diff_w_o[j].astype(BF16)
            qscale = DA_HEAD_DIM ** -0.5
            lqk = jnp.stack([diff_lq1[j], diff_lk1[j], diff_lq2[j], diff_lk2[j]], axis=0)
            lqk = jnp.pad(lqk, ((0, 4), (0, 2 * DA_HEAD_DIM - lqk.shape[1])))
            gsub = diff_norm[j].reshape(1, 2 * DA_HEAD_DIM)
            q, k, v = _fused_proj(xp, norm_mix[i], mod_p, w,
                                  [(nh, "plain", qscale), (nh, "plain", 1.0), (nh, "plain", 1.0)],
                                  [BF16, F32, F32], tm=tmp, name="diff_proj_ctx")
            outs["diff_k"] = k.reshape(bp, 1, lp, DA_HEADS, 2, DA_HEAD_DIM)
            outs["diff_v"] = v.reshape(bp, 1, lp, DA_HEADS, 2 * DA_HEAD_DIM)
            o = _diff_attn(q, k, v, None, None, lqk, gsub, tq=min(lp, 256), kc=_key_chunk(lp), lam_init=lam_init)
            xp = _out_proj(o, w_o, xp, mod_p, tm=tmp, name="diff_out_ctx")
            rope = _rope_tables(ls, DA_HEAD_DIM)
            q, k, v = _fused_proj(xs, norm_mix[i], mod_s, w,
                                  [(nh, "rope", qscale), (nh, "rope", 1.0), (nh, "plain", 1.0)],
                                  [BF16, BF16, BF16], tm=tms, rope=rope, name="diff_proj_lat")
            n_ctx = cache_diff_k.shape[2]
            kctx = cache_diff_k[:, j].reshape(bs, n_ctx, nh).astype(BF16)
            vctx = cache_diff_v[:, j].reshape(bs, n_ctx, nh).astype(BF16)
            o = _diff_attn(q, k, v, kctx, vctx, lqk, gsub, tq=min(ls, 512), kc=_key_chunk(ls + n_ctx),
                           lam_init=lam_init)
            xs = _out_proj(o, w_o, xs, mod_s, tm=tms, name="diff_out_lat")
        else:
            gn = SSD_GROUPS * SSD_STATE
            nxbc = SSD_D_INNER + 2 * gn
            w_in = ssd_w_in[j]
            w = jnp.concatenate([w_in[:, :SSD_D_INNER + nxbc], _pad_cols(w_in[:, SSD_D_INNER + nxbc:], LANES)],
                                axis=1).astype(BF16)
            segs = [(SSD_D_INNER, "plain", 1.0), (SSD_D_INNER, "conv_silu", 1.0), (2 * gn, "conv_silu", 1.0),
                    (LANES, "plain", 1.0)]
            seg_dtypes = [BF16, F32, BF16, F32]
            cw = _conv_pack(ssd_conv_w[j], ssd_conv_b[j])
            zpad = jnp.zeros((LANES - 2 * SSD_HEADS,), F32)
            zh = jnp.zeros((SSD_HEADS,), F32)
            bias = jnp.concatenate([ssd_dt_bias_f[j], ssd_dt_bias_b[j], zpad]).reshape(1, LANES)
            a_f = jnp.concatenate([-jnp.exp(ssd_a_log_f[j]), zh, zpad]).reshape(1, LANES)
            a_b = jnp.concatenate([zh, -jnp.exp(ssd_a_log_b[j]), zpad]).reshape(1, LANES)
            dsk = jnp.concatenate([ssd_d[j], zh, zpad]).reshape(1, LANES)
            w_out = ssd_w_out[j].astype(BF16)
            rows = SSD_HEADS * SSD_HEAD_DIM
            for stream in ("p", "s"):
                if stream == "p":
                    x, mod, tm, bsz = xp, mod_p, tmp, bp
                    s0_f = jnp.zeros((bp, rows, SSD_STATE), F32)
                    s0_b = s0_f
                else:
                    x, mod, tm, bsz = xs, mod_s, tms, bs
                    s0_f = state_ssd_fwd[:, j].reshape(bs, rows, SSD_STATE)
                    s0_b = state_ssd_bwd[:, j].reshape(bs, rows, SSD_STATE)
                z, xc, bc, dt = _fused_proj(x, norm_mix[i], mod, w, segs, seg_dtypes, tm=tm, conv_w=cw,
                                            name="ssd_proj_" + stream)
                y_f, s_f = _ssd_scan(xc, bc, dt, bias, a_f, dsk, s0_f, reverse=False, add_skip=True)
                y_b, s_b = _ssd_scan(xc, bc, dt, bias, a_b, dsk, s0_b, reverse=True, add_skip=False)
                x = _ssd_out(y_f, y_b, z, ssd_norm[j], w_out, x, mod, tm=tm)
                if stream == "p":
                    xp = x
                    outs["ssd_f"] = s_f.reshape(bp, 1, SSD_HEADS, SSD_HEAD_DIM, SSD_STATE)
                    outs["ssd_b"] = s_b.reshape(bp, 1, SSD_HEADS, SSD_HEAD_DIM, SSD_STATE)
                else:
                    xs = x

        w_up = ffn_w_up[i].astype(BF16)
        cw = _conv_pack(ffn_conv_w[i], ffn_conv_b[i])
        wd = ffn_w_down[i].astype(BF16)
        last = i == DEPTH - 1
        xp = _ffn(xp, norm_ffn[i], mod_p, w_up, cw, wd, final_norm, tm=tmp, final_norm=last, name="ffn_p")
        xs = _ffn(xs, norm_ffn[i], mod_s, w_up, cw, wd, final_norm, tm=min(ls, 2 * tms), final_norm=last,
                  name="ffn_s")

    return (xp, xs, outs["win_k"], outs["win_v"], outs["gla_f"], outs["gla_b"],
            outs["diff_k"], outs["diff_v"], outs["ssd_f"], outs["ssd_b"])
```

```python
import functools
import math

import jax
import jax.numpy as jnp
from jax import lax
from jax.experimental import pallas as pl
from jax.experimental.pallas import tpu as pltpu

F32 = jnp.float32
BF16 = jnp.bfloat16

D_MODEL = 1024
DEPTH = 4
GRID_W = 64
EPS = 1e-6
ROPE_BASE = 10000.0

WA_HEADS = 16
WA_KV_HEADS = 4
WA_GROUP = 4
WA_HEAD_DIM = 64
WINDOW = 128

GLA_HEADS = 4
GLA_DK = 128
GLA_DV = 256
GLA_RANK = 16
GLA_TAU = 16.0
GLA_CHUNK = 64
GLA_CUM_ROWS = 256

DA_HEADS = 8
DA_HEAD_DIM = 64

SSD_D_INNER = 2048
SSD_HEAD_DIM = 64
SSD_HEADS = 32
SSD_GROUPS = 4
SSD_STATE = 128
SSD_CHUNK = 128

D_FF = 2816
FF_CHUNK = 256
FF_GROUP = 4

VMEM_LIMIT_BYTES = 56 * 1024 * 1024
SUBLANES = 8
LANES = 128
NEG_BIG = -1e30
LOG2_E = 1.4426950408889634
DIFF_STRIP = 16
SOFTMAX_STRIP = 32


def _cparams(sem):
    return pltpu.CompilerParams(dimension_semantics=sem, vmem_limit_bytes=VMEM_LIMIT_BYTES)


def _sigmoid(x):
    return 1.0 / (1.0 + jnp.exp(-x))


def _silu(x):
    return x * _sigmoid(x)


def _softplus(x):
    return jnp.maximum(x, 0.0) + jnp.log1p(jnp.exp(-jnp.abs(x)))


def _norm_mod(x, gamma, shift, scale):
    ms = jnp.mean(x * x, axis=-1, keepdims=True)
    y = (x * lax.rsqrt(ms + EPS)) * gamma
    return y * (1.0 + scale) + shift


def _split3(x):
    hi = x.astype(BF16)
    r1 = x - hi.astype(F32)
    mid = r1.astype(BF16)
    lo = (r1 - mid.astype(F32)).astype(BF16)
    return hi, mid, lo


def _exact_dot_left01(m01, x):
    hi, mid, lo = _split3(x)
    d = lambda p: jnp.dot(m01, p, preferred_element_type=F32)
    return d(hi) + d(mid) + d(lo)


def _dot_nt(a, b):
    return lax.dot_general(a, b, (((1,), (1,)), ((), ())), preferred_element_type=F32)


def _ada_kernel(c_ref, w_ref, b_ref, o_ref):
    h = _silu(c_ref[...]).astype(BF16)
    o_ref[0] = jnp.dot(h, w_ref[0].astype(BF16), preferred_element_type=F32) + b_ref[0]


def _ada_call(cond, ada_w, ada_b):
    rows = cond.shape[0]
    n = ada_w.shape[-1]
    tn = 1536
    return pl.pallas_call(
        _ada_kernel,
        grid=(DEPTH, n // tn),
        in_specs=[
            pl.BlockSpec((rows, D_MODEL), lambda l, j: (0, 0)),
            pl.BlockSpec((1, D_MODEL, tn), lambda l, j: (l, 0, j)),
            pl.BlockSpec((1, 1, tn), lambda l, j: (l, 0, j)),
        ],
        out_specs=pl.BlockSpec((1, rows, tn), lambda l, j: (l, 0, j)),
        out_shape=jax.ShapeDtypeStruct((DEPTH, rows, n), F32),
        compiler_params=_cparams(("parallel", "parallel")),
        name="ada_mod",
    )(cond, ada_w, ada_b.reshape(DEPTH, 1, n))


def _rope_apply(y, cos, sin):
    lane = lax.broadcasted_iota(jnp.int32, (1, LANES), 1)
    first = (lane % 32) < 16
    partner = jnp.where(first, pltpu.roll(y, LANES - 16, 1), pltpu.roll(y, 16, 1))
    return y * cos + partner * sin


def _fused_proj_kernel(*refs, tm, segs, has_conv, has_rope, n_tiles, chunk):
    it = iter(refs)
    x_ref = next(it)
    if has_conv:
        xp_ref = next(it)
        xn_ref = next(it)
    g_ref = next(it)
    mod_ref = next(it)
    w_ref = next(it)
    if has_rope:
        cos_ref = next(it)
        sin_ref = next(it)
    if has_conv:
        cw_ref = next(it)
    out_refs = [next(it) for _ in segs]
    h_ref = next(it)

    i = pl.program_id(1)
    gamma = g_ref[...]
    shift = mod_ref[0, 0:1, :]
    scale = mod_ref[0, 1:2, :]
    off = SUBLANES if has_conv else 0
    h_ref[off:off + tm, :] = _norm_mod(x_ref[0], gamma, shift, scale).astype(BF16)
    if has_conv:
        hp = _norm_mod(xp_ref[0], gamma, shift, scale)
        hn = _norm_mod(xn_ref[0], gamma, shift, scale)
        h_ref[0:SUBLANES, :] = jnp.where(i > 0, hp, 0.0).astype(BF16)
        h_ref[off + tm:off + tm + SUBLANES, :] = jnp.where(i < n_tiles - 1, hn, 0.0).astype(BF16)

    col = 0
    conv_col = 0
    for seg, o_ref in zip(segs, out_refs):
        width, epi, qscale = seg
        for c0 in range(0, width, chunk):
            wc = min(chunk, width - c0)
            w = w_ref[:, col + c0:col + c0 + wc]
            if epi == "conv_silu":
                u = jnp.dot(h_ref[...], w, preferred_element_type=F32)
                rows = tm + 2 * SUBLANES
                up = pltpu.roll(u, 1, 0)[off:off + tm]
                un = pltpu.roll(u, rows - 1, 0)[off:off + tm]
                uc = u[off:off + tm]
                cw = cw_ref[:, conv_col + c0:conv_col + c0 + wc]
                y = cw[0:1] * up + cw[1:2] * uc + cw[2:3] * un + cw[3:4]
                y = _silu(y)
            else:
                y = jnp.dot(h_ref[off:off + tm, :], w, preferred_element_type=F32)
                if qscale != 1.0:
                    y = y * qscale
                if epi == "rope":
                    cos = cos_ref[...]
                    sin = sin_ref[...]
                    y = jnp.concatenate(
                        [_rope_apply(y[:, k:k + LANES], cos, sin) for k in range(0, wc, LANES)], axis=1)
            o_ref[0, :, c0:c0 + wc] = y.astype(o_ref.dtype)
        col += width
        if epi == "conv_silu":
            conv_col += width


def _fused_proj(x, gamma, mod, w, segs, out_dtypes, *, tm, rope=None, conv_w=None, name):
    b, l, d = x.shape
    n_tiles = l // tm
    has_conv = conv_w is not None
    has_rope = rope is not None
    per_batch = mod.shape[0] > 1
    n_total = w.shape[1]
    bpt = tm // SUBLANES
    nblk8 = l // SUBLANES

    in_specs = [pl.BlockSpec((1, tm, d), lambda bi, i: (bi, i, 0))]
    args = [x]
    if has_conv:
        in_specs.append(pl.BlockSpec((1, SUBLANES, d), lambda bi, i: (bi, jnp.maximum(i * bpt - 1, 0), 0)))
        in_specs.append(pl.BlockSpec((1, SUBLANES, d), lambda bi, i: (bi, jnp.minimum((i + 1) * bpt, nblk8 - 1), 0)))
        args += [x, x]
    in_specs.append(pl.BlockSpec((1, d), lambda bi, i: (0, 0)))
    args.append(gamma.reshape(1, d))
    in_specs.append(pl.BlockSpec((1, 8, d), (lambda bi, i: (bi, 0, 0)) if per_batch else (lambda bi, i: (0, 0, 0))))
    args.append(mod)
    in_specs.append(pl.BlockSpec((d, n_total), lambda bi, i: (0, 0)))
    args.append(w)
    if has_rope:
        in_specs.append(pl.BlockSpec((tm, LANES), lambda bi, i: (i, 0)))
        in_specs.append(pl.BlockSpec((tm, LANES), lambda bi, i: (i, 0)))
        args += [rope[0], rope[1]]
    if has_conv:
        in_specs.append(pl.BlockSpec(conv_w.shape, lambda bi, i: (0, 0)))
        args.append(conv_w)

    out_specs = [pl.BlockSpec((1, tm, s[0]), lambda bi, i: (bi, i, 0)) for s in segs]
    out_shape = [jax.ShapeDtypeStruct((b, l, s[0]), dt) for s, dt in zip(segs, out_dtypes)]
    hrows = tm + (2 * SUBLANES if has_conv else 0)
    kern = functools.partial(_fused_proj_kernel, tm=tm, segs=tuple(segs), has_conv=has_conv,
                             has_rope=has_rope, n_tiles=n_tiles, chunk=512)
    return pl.pallas_call(
        kern,
        grid=(b, n_tiles),
        in_specs=in_specs,
        out_specs=out_specs,
        out_shape=out_shape,
        scratch_shapes=[pltpu.VMEM((hrows, d), BF16)],
        compiler_params=_cparams(("parallel", "parallel")),
        name=name,
    )(*args)


def _out_proj_kernel(o_ref, w_ref, x_ref, mod_ref, out_ref, *, gate_row):
    acc = jnp.dot(o_ref[0].astype(BF16), w_ref[...], preferred_element_type=F32)
    out_ref[0] = x_ref[0] + mod_ref[0, gate_row:gate_row + 1, :] * acc


def _mod_spec(mod):
    if mod.shape[0] > 1:
        return pl.BlockSpec((1, 8, D_MODEL), lambda bi, i: (bi, 0, 0))
    return pl.BlockSpec((1, 8, D_MODEL), lambda bi, i: (0, 0, 0))


def _out_proj(o, w, x, mod, *, tm, name):
    b, l, k = o.shape
    return pl.pallas_call(
        functools.partial(_out_proj_kernel, gate_row=2),
        grid=(b, l // tm),
        in_specs=[
            pl.BlockSpec((1, tm, k), lambda bi, i: (bi, i, 0)),
            pl.BlockSpec((k, D_MODEL), lambda bi, i: (0, 0)),
            pl.BlockSpec((1, tm, D_MODEL), lambda bi, i: (bi, i, 0)),
            _mod_spec(mod),
        ],
        out_specs=pl.BlockSpec((1, tm, D_MODEL), lambda bi, i: (bi, i, 0)),
        out_shape=jax.ShapeDtypeStruct((b, l, D_MODEL), F32),
        compiler_params=_cparams(("parallel", "parallel")),
        name=name,
    )(o, w, x, mod)


def _ffn_kernel(x_ref, xp_ref, xn_ref, g_ref, mod_ref, wu_ref, cw_ref, wd_ref, fg_ref,
                out_ref, h_ref, hnat_ref, acc_ref, act_ref, *, tm, n_tiles, final_norm):
    i = pl.program_id(1)
    gamma = g_ref[...]
    shift = mod_ref[0, 3:4, :]
    scale = mod_ref[0, 4:5, :]
    nv = tm // SUBLANES
    pitch = nv + SUBLANES
    rows = tm + 2 * SUBLANES
    n_slab = D_MODEL // LANES

    h_nat = _norm_mod(x_ref[0], gamma, shift, scale)
    for s in range(SUBLANES):
        for k in range(n_slab):
            hnat_ref[k, s * pitch:s * pitch + nv, :] = h_nat[s * nv:(s + 1) * nv, k * LANES:(k + 1) * LANES]

    def gather_rows(ref, start, stride):
        return jnp.concatenate([ref[k, pl.ds(start, SUBLANES, stride=stride), :] for k in range(n_slab)], axis=1)

    for v in range(0, nv, 2):
        pair = jnp.concatenate([gather_rows(hnat_ref, v, pitch), gather_rows(hnat_ref, v + 1, pitch)], axis=0)
        h_ref[v * SUBLANES:(v + 2) * SUBLANES, :] = pair.astype(BF16)
    hp = jnp.where(i > 0, _norm_mod(xp_ref[0], gamma, shift, scale), 0.0)
    hn = jnp.where(i < n_tiles - 1, _norm_mod(xn_ref[0], gamma, shift, scale), 0.0)
    h_ref[tm:rows, :] = jnp.concatenate([hp, hn], axis=0).astype(BF16)

    sub = lax.broadcasted_iota(jnp.int32, (SUBLANES, 1), 0)

    def conv(u, cw):
        main = u[0:tm]
        first_prev = pltpu.roll(jnp.where(sub == SUBLANES - 1, u[tm:tm + SUBLANES], u[tm - SUBLANES:tm]), 1, 0)
        last_next = pltpu.roll(jnp.where(sub == 0, u[tm + SUBLANES:rows], u[0:SUBLANES]), SUBLANES - 1, 0)
        prev = jnp.concatenate([first_prev, u[0:tm - SUBLANES]], axis=0)
        nxt = jnp.concatenate([u[SUBLANES:tm], last_next], axis=0)
        return cw[0:1] * prev + cw[1:2] * main + cw[2:3] * nxt + cw[3:4]

    n_chunks = D_FF // FF_CHUNK
    group_start = 0
    for c in range(n_chunks):
        gs = slice(c * FF_CHUNK, (c + 1) * FF_CHUNK)
        vs = slice(D_FF + c * FF_CHUNK, D_FF + (c + 1) * FF_CHUNK)
        h = h_ref[...]
        ug = jnp.dot(h, wu_ref[:, gs], preferred_element_type=F32)
        uv = jnp.dot(h, wu_ref[:, vs], preferred_element_type=F32)
        a = _silu(conv(ug, cw_ref[:, gs])) * conv(uv, cw_ref[:, vs])
        act_ref[:, gs] = a.astype(BF16)
        if (c + 1) % FF_GROUP == 0 or c == n_chunks - 1:
            ks = slice(group_start * FF_CHUNK, (c + 1) * FF_CHUNK)
            part = jnp.dot(act_ref[:, ks], wd_ref[ks, :], preferred_element_type=F32)
            for k in range(n_slab):
                if group_start == 0:
                    acc_ref[k] = part[:, k * LANES:(k + 1) * LANES]
                else:
                    acc_ref[k] += part[:, k * LANES:(k + 1) * LANES]
            group_start = c + 1
    gate = mod_ref[0, 5:6, :]
    for j in range(nv):
        s, v0 = divmod(j * SUBLANES, nv)
        rs = slice(j * SUBLANES, (j + 1) * SUBLANES)
        y = x_ref[0, rs, :] + gate * gather_rows(acc_ref, v0 * SUBLANES + s, SUBLANES)
        if final_norm:
            ms = jnp.mean(y * y, axis=-1, keepdims=True)
            y = (y * lax.rsqrt(ms + EPS)) * fg_ref[...]
        out_ref[0, rs, :] = y


def _ffn(x, gamma, mod, w_up, cw, w_down, final_gamma, *, tm, final_norm, name):
    b, l, d = x.shape
    n_tiles = l // tm
    bpt = tm // SUBLANES
    nblk8 = l // SUBLANES
    const2 = lambda bi, i: (0, 0)
    rows = tm + 2 * SUBLANES
    kern = functools.partial(_ffn_kernel, tm=tm, n_tiles=n_tiles, final_norm=final_norm)
    return pl.pallas_call(
        kern,
        grid=(b, n_tiles),
        in_specs=[
            pl.BlockSpec((1, tm, d), lambda bi, i: (bi, i, 0)),
            pl.BlockSpec((1, SUBLANES, d), lambda bi, i: (bi, jnp.maximum(i * bpt - 1, 0), 0)),
            pl.BlockSpec((1, SUBLANES, d), lambda bi, i: (bi, jnp.minimum((i + 1) * bpt, nblk8 - 1), 0)),
            pl.BlockSpec((1, d), const2),
            _mod_spec(mod),
            pl.BlockSpec(w_up.shape, const2, pipeline_mode=pl.Buffered(1)),
            pl.BlockSpec(cw.shape, const2),
            pl.BlockSpec(w_down.shape, const2, pipeline_mode=pl.Buffered(1)),
            pl.BlockSpec((1, d), const2),
        ],
        out_specs=pl.BlockSpec((1, tm, d), lambda bi, i: (bi, i, 0)),
        out_shape=jax.ShapeDtypeStruct((b, l, d), F32),
        scratch_shapes=[pltpu.VMEM((rows, d), BF16),
                        pltpu.VMEM((d // LANES, SUBLANES * (tm // SUBLANES + SUBLANES), LANES), F32),
                        pltpu.VMEM((d // LANES, tm, LANES), F32),
                        pltpu.VMEM((tm, D_FF), BF16)],
        compiler_params=_cparams(("parallel", "parallel")),
        name=name,
    )(x, x, x, gamma.reshape(1, d), mod, w_up, cw, w_down, final_gamma.reshape(1, d))


def _gqa_core(sink_ref, q_ref, kcat, vcat, bias, o_ref, tq):
    lane_head = lax.broadcasted_iota(jnp.int32, (1, WA_KV_HEADS * WA_HEAD_DIM), 1) // WA_HEAD_DIM
    width = WA_KV_HEADS * WA_HEAD_DIM

    def scores(g):
        qg = q_ref[0, :, g * width:(g + 1) * width]
        q4 = jnp.concatenate([jnp.where(lane_head == h, qg, jnp.zeros_like(qg)) for h in range(WA_KV_HEADS)], axis=0)
        return _dot_nt(q4, kcat)

    s_next = scores(0)
    for g in range(WA_GROUP):
        s = s_next
        if g + 1 < WA_GROUP:
            s_next = scores(g + 1)
        p_parts = []
        inv_parts = []
        for h in range(WA_KV_HEADS):
            sink = sink_ref[h * WA_GROUP + g] * LOG2_E
            for t in range(0, tq, SOFTMAX_STRIP):
                st = s[h * tq + t:h * tq + t + SOFTMAX_STRIP]
                if bias is not None:
                    st = st + bias[t:t + SOFTMAX_STRIP]
                m = jnp.maximum(jnp.max(st, axis=-1, keepdims=True), sink)
                p = jnp.exp2(st - m)
                inv_parts.append(1.0 / (jnp.sum(p, axis=-1, keepdims=True) + jnp.exp2(sink - m)))
                p_parts.append(p.astype(BF16))
        p = jnp.concatenate(p_parts, axis=0)
        o4 = jnp.dot(p, vcat, preferred_element_type=F32) * jnp.concatenate(inv_parts, axis=0)
        og = jnp.zeros((tq, width), F32)
        for h in range(WA_KV_HEADS):
            og = og + jnp.where(lane_head == h, o4[h * tq:(h + 1) * tq], 0.0)
        o_ref[0, :, g * width:(g + 1) * width] = og.astype(o_ref.dtype)


def _win_ctx_kernel(sink_ref, q_ref, k_ref, v_ref, o_ref, *, tq):
    _gqa_core(sink_ref, q_ref, k_ref[0].astype(BF16), v_ref[0].astype(BF16), None, o_ref, tq)


def _win_ctx_attn(q, k, v, sink, *, tq):
    b, l, _ = q.shape
    kvw = WA_KV_HEADS * WA_HEAD_DIM
    return pl.pallas_call(
        functools.partial(_win_ctx_kernel, tq=tq),
        grid=(b, l // tq),
        in_specs=[
            pl.BlockSpec(memory_space=pltpu.SMEM),
            pl.BlockSpec((1, tq, D_MODEL), lambda bi, i: (bi, i, 0)),
            pl.BlockSpec((1, l, kvw), lambda bi, i: (bi, 0, 0)),
            pl.BlockSpec((1, l, kvw), lambda bi, i: (bi, 0, 0)),
        ],
        out_specs=pl.BlockSpec((1, tq, D_MODEL), lambda bi, i: (bi, i, 0)),
        out_shape=jax.ShapeDtypeStruct((b, l, D_MODEL), BF16),
        compiler_params=_cparams(("parallel", "parallel")),
        name="win_ctx_attn",
    )(sink, q, k, v)


def _win_lat_kernel(sink_ref, q_ref, kp_ref, kc_ref, kn_ref, vp_ref, vc_ref, vn_ref, kx_ref, vx_ref, o_ref,
                    kcat, vcat, *, tq, seq_len, n_ctx):
    i = pl.program_id(1)
    halo = WINDOW
    n_lat = tq + 2 * halo
    nk = n_lat + n_ctx
    for src, dst in ((kp_ref, kcat), (vp_ref, vcat)):
        dst[0:halo] = src[0]
    for src, dst in ((kc_ref, kcat), (vc_ref, vcat)):
        dst[halo:halo + tq] = src[0]
    for src, dst in ((kn_ref, kcat), (vn_ref, vcat)):
        dst[halo + tq:n_lat] = src[0]
    for src, dst in ((kx_ref, kcat), (vx_ref, vcat)):
        dst[n_lat:nk] = src[0]
    r = lax.broadcasted_iota(jnp.int32, (tq, nk), 0)
    c = lax.broadcasted_iota(jnp.int32, (tq, nk), 1)
    kpos = i * tq - halo + c
    ok = (c >= n_lat) | ((kpos >= 0) & (kpos < seq_len) & (jnp.abs(r + halo - c) <= WINDOW))
    bias = jnp.where(ok, 0.0, NEG_BIG).astype(F32)
    _gqa_core(sink_ref, q_ref, kcat[...], vcat[...], bias, o_ref, tq)


def _win_lat_attn(q, k, v, kctx, vctx, sink, *, tq):
    b, l, _ = q.shape
    halo = WINDOW
    assert tq % halo == 0 and l % tq == 0
    nq = l // tq
    per = tq // halo
    n_halo_blocks = l // halo
    n_ctx = kctx.shape[1]
    kvw = WA_KV_HEADS * WA_HEAD_DIM
    n_lat = tq + 2 * halo
    prev = lambda bi, i: (bi, jnp.maximum(i * per - 1, 0), 0)
    cur = lambda bi, i: (bi, i, 0)
    nxt = lambda bi, i: (bi, jnp.minimum((i + 1) * per, n_halo_blocks - 1), 0)
    halo_spec = lambda f: pl.BlockSpec((1, halo, kvw), f)
    cur_spec = pl.BlockSpec((1, tq, kvw), cur)
    return pl.pallas_call(
        functools.partial(_win_lat_kernel, tq=tq, seq_len=l, n_ctx=n_ctx),
        grid=(b, nq),
        in_specs=[
            pl.BlockSpec(memory_space=pltpu.SMEM),
            pl.BlockSpec((1, tq, D_MODEL), cur),
            halo_spec(prev), cur_spec, halo_spec(nxt),
            halo_spec(prev), cur_spec, halo_spec(nxt),
            pl.BlockSpec((1, n_ctx, kvw), lambda bi, i: (bi, 0, 0)),
            pl.BlockSpec((1, n_ctx, kvw), lambda bi, i: (bi, 0, 0)),
        ],
        out_specs=pl.BlockSpec((1, tq, D_MODEL), cur),
        out_shape=jax.ShapeDtypeStruct((b, l, D_MODEL), BF16),
        scratch_shapes=[pltpu.VMEM((n_lat + n_ctx, kvw), BF16), pltpu.VMEM((n_lat + n_ctx, kvw), BF16)],
        compiler_params=_cparams(("parallel", "parallel")),
        name="win_lat_attn",
    )(sink, q, k, k, k, v, v, v, kctx, vctx)


def _diff_kernel(lqk_ref, gsub_ref, q_ref, k_ref, v_ref, *rest, tq, kc, n_keys, has_ctx, lam_init):
    if has_ctx:
        kx_ref, vx_ref, o_ref, kall, vall, m_ref, acc_ref, s_a, s_b, p_ref, alpha_ref = rest
    else:
        o_ref, kall, vall, m_ref, acc_ref, s_a, s_b, p_ref, alpha_ref = rest
    hd = DA_HEAD_DIM
    lam = (jnp.exp(jnp.sum(lqk_ref[0:1, :] * lqk_ref[1:2, :], axis=-1, keepdims=True))
           - jnp.exp(jnp.sum(lqk_ref[2:3, :] * lqk_ref[3:4, :], axis=-1, keepdims=True)) + lam_init)
    hw = 2 * hd

    @pl.when(pl.program_id(2) == 0)
    def _():
        lat = k_ref.shape[1]
        kall[0:lat, :] = k_ref[0].astype(BF16)
        vall[0:lat, 0:hw] = v_ref[0].astype(BF16)
        if has_ctx:
            kall[lat:n_keys, :] = kx_ref[0]
            vall[lat:n_keys, 0:hw] = vx_ref[0]
        vall[:, hw:2 * hw] = jnp.ones((n_keys, hw), BF16)

    q = q_ref[0]
    lane = lax.broadcasted_iota(jnp.int32, (1, hw), 1)
    zero = jnp.zeros_like(q)
    q2 = jnp.concatenate([jnp.where(lane < hd, q, zero), jnp.where(lane >= hd, q, zero)], axis=0)

    n_chunks = n_keys // kc
    s_bufs = (s_a, s_b)
    s_bufs[0][...] = _dot_nt(q2, kall[0:kc, :])
    for j in range(n_chunks):
        if j + 1 < n_chunks:
            s_bufs[(j + 1) % 2][...] = _dot_nt(q2, kall[(j + 1) * kc:(j + 2) * kc, :])
        s_ref = s_bufs[j % 2]
        for t in range(0, 2 * tq, DIFF_STRIP):
            rows = slice(t, t + DIFF_STRIP)
            st = s_ref[rows, :]
            m_chunk = jnp.max(st, axis=-1, keepdims=True)
            if j == 0:
                m_new = jnp.broadcast_to(m_chunk, (DIFF_STRIP, LANES))
            else:
                m_old = m_ref[rows, :]
                m_new = jnp.maximum(m_old, m_chunk)
                alpha_ref[rows, :] = jnp.exp2(m_old - m_new)
            p_ref[rows, :] = jnp.exp2(st - jnp.concatenate([m_new] * (kc // LANES), axis=1)).astype(BF16)
            if j + 1 < n_chunks:
                m_ref[rows, :] = m_new
        pv = jnp.dot(p_ref[...], vall[j * kc:(j + 1) * kc, :], preferred_element_type=F32)
        if j == 0:
            acc_ref[...] = pv
        else:
            alpha = alpha_ref[...]
            acc_ref[...] = jnp.concatenate([alpha, alpha], axis=1) * acc_ref[...] + pv
    acc = acc_ref[...]
    o2 = acc[:, 0:hw] / acc[:, hw:2 * hw]
    o = o2[0:tq] - lam * o2[tq:2 * tq]
    ms = jnp.mean(o * o, axis=-1, keepdims=True)
    o = (o * lax.rsqrt(ms + EPS)) * gsub_ref[...] * (1.0 - lam_init)
    o_ref[0] = o.astype(o_ref.dtype)


def _diff_attn(q, k, v, kctx, vctx, lqk, gsub, *, tq, kc, lam_init):
    b, l, _ = q.shape
    hw = 2 * DA_HEAD_DIM
    has_ctx = kctx is not None
    in_specs = [
        pl.BlockSpec((8, hw), lambda bi, h, i: (0, 0)),
        pl.BlockSpec((1, hw), lambda bi, h, i: (0, 0)),
        pl.BlockSpec((1, tq, hw), lambda bi, h, i: (bi, i, h)),
        pl.BlockSpec((1, l, hw), lambda bi, h, i: (bi, 0, h)),
        pl.BlockSpec((1, l, hw), lambda bi, h, i: (bi, 0, h)),
    ]
    args = [lqk, gsub, q, k, v]
    n_keys = l
    if has_ctx:
        n_ctx = kctx.shape[1]
        n_keys = l + n_ctx
        in_specs.append(pl.BlockSpec((1, n_ctx, hw), lambda bi, h, i: (bi, 0, h)))
        in_specs.append(pl.BlockSpec((1, n_ctx, hw), lambda bi, h, i: (bi, 0, h)))
        args += [kctx, vctx]
    assert n_keys % kc == 0 and kc % LANES == 0
    kern = functools.partial(_diff_kernel, tq=tq, kc=kc, n_keys=n_keys, has_ctx=has_ctx, lam_init=lam_init)
    return pl.pallas_call(
        kern,
        grid=(b, DA_HEADS, l // tq),
        in_specs=in_specs,
        out_specs=pl.BlockSpec((1, tq, hw), lambda bi, h, i: (bi, i, h)),
        out_shape=jax.ShapeDtypeStruct((b, l, D_MODEL), BF16),
        scratch_shapes=[pltpu.VMEM((n_keys, hw), BF16), pltpu.VMEM((n_keys, 2 * hw), BF16),
                        pltpu.VMEM((2 * tq, LANES), F32), pltpu.VMEM((2 * tq, 2 * hw), F32),
                        pltpu.VMEM((2 * tq, kc), F32), pltpu.VMEM((2 * tq, kc), F32),
                        pltpu.VMEM((2 * tq, kc), BF16), pltpu.VMEM((2 * tq, LANES), F32)],
        compiler_params=_cparams(("parallel", "parallel", "arbitrary")),
        name="diff_lat_attn" if has_ctx else "diff_ctx_attn",
    )(*args)


def _gla_scan_kernel(qf_ref, kf_ref, vf_ref, zf_ref, qb_ref, kb_ref, vb_ref, zb_ref, w2_ref, b2_ref, s0_ref,
                     of_ref, ob_ref, sf_ref, st_scr, qd_scr, kd_scr, kr_scr, vt_scr, mt_scr, el_scr,
                     *, blk, n_blocks):
    i = pl.program_id(1)
    ch = GLA_CHUNK
    nk = GLA_HEADS * GLA_DK
    n_sub = blk // ch
    dirs = ((qf_ref, kf_ref, vf_ref, zf_ref, of_ref), (qb_ref, kb_ref, vb_ref, zb_ref, ob_ref))

    @pl.when(i == 0)
    def _():
        for d in range(2):
            for h in range(GLA_HEADS):
                st_scr[d, h] = jnp.transpose(s0_ref[d, 0, h])

    cs = min(blk, GLA_CUM_ROWS)
    r = lax.broadcasted_iota(jnp.int32, (cs, cs), 0)
    c = lax.broadcasted_iota(jnp.int32, (cs, cs), 1)
    same_chunk = (r // ch) == (c // ch)
    rr = lax.broadcasted_iota(jnp.int32, (ch, ch), 0)
    cc = lax.broadcasted_iota(jnp.int32, (ch, ch), 1)
    keeps = (rr >= cc, rr <= cc)
    edges = (ch - 1, 0)

    for d, (q_ref, k_ref, v_ref, zl_ref, _) in enumerate(dirs):
        tri = (same_chunk & ((r <= c) if d == 1 else (r >= c))).astype(BF16)
        z = jnp.dot(zl_ref[0].astype(BF16), w2_ref[d], preferred_element_type=F32) + b2_ref[d]
        logg = (jnp.minimum(z, 0.0) - jnp.log1p(jnp.exp(-jnp.abs(z)))) / GLA_TAU
        cum = jnp.concatenate([_exact_dot_left01(tri, logg[t:t + cs]) for t in range(0, blk, cs)], axis=0)
        lasts = [cum[s * ch + edges[d]:s * ch + edges[d] + 1, :] for s in range(n_sub)]
        blast = jnp.concatenate([jnp.broadcast_to(b, (ch, nk)) for b in lasts], axis=0)
        k = k_ref[0]
        qd_scr[d] = ((q_ref[0] * (GLA_DK ** -0.5)) * jnp.exp(cum)).astype(BF16)
        kd_scr[d] = (k * jnp.exp(-cum)).astype(BF16)
        kr_scr[d] = (k * jnp.exp(blast - cum)).astype(BF16)
        for s in range(n_sub):
            el_scr[d, s:s + 1, :] = jnp.exp(lasts[s])
        vt_scr[d] = jnp.transpose(v_ref[0].astype(F32))

    lane_chunk = lax.broadcasted_iota(jnp.int32, (1, 2 * ch), 1) // ch
    for s in range(n_sub):
        rows = slice(s * ch, (s + 1) * ch)
        pair = slice((s // 2) * 2 * ch, (s // 2 + 1) * 2 * ch)
        for h in range(GLA_HEADS):
            ks = slice(h * GLA_DK, (h + 1) * GLA_DK)
            vs = slice(h * GLA_DV, (h + 1) * GLA_DV)
            for d, (_, _, v_ref, _, o_ref) in enumerate(dirs):
                att = jnp.where(keeps[d], _dot_nt(qd_scr[d, rows, ks], kd_scr[d, rows, ks]), 0.0)
                o_ref[0, rows, vs] = jnp.dot(att.astype(BF16), v_ref[0, rows, vs].astype(BF16),
                                             preferred_element_type=F32)
                vt = jnp.where(lane_chunk == s % 2, vt_scr[d, vs, pair], 0.0).astype(BF16)
                mt_scr[d, s, h] = jnp.dot(vt, kr_scr[d, pair, ks], preferred_element_type=F32)

    for step in range(n_sub):
        for h in range(GLA_HEADS):
            ks = slice(h * GLA_DK, (h + 1) * GLA_DK)
            vs = slice(h * GLA_DV, (h + 1) * GLA_DV)
            for d, (_, _, _, _, o_ref) in enumerate(dirs):
                s = step if d == 0 else n_sub - 1 - step
                rows = slice(s * ch, (s + 1) * ch)
                st = st_scr[d, h]
                o_ref[0, rows, vs] += _dot_nt(qd_scr[d, rows, ks], st.astype(BF16))
                st_scr[d, h] = st * el_scr[d, s:s + 1, ks] + mt_scr[d, s, h]

    @pl.when(i == n_blocks - 1)
    def _():
        for d in range(2):
            for h in range(GLA_HEADS):
                sf_ref[d, 0, h] = jnp.transpose(st_scr[d, h])


def _gla_scan(q, k, v, zl, w2, b2, s0, *, blk):
    b, l, nk = q.shape
    nv = v.shape[-1]
    n_blocks = l // blk
    n_sub = blk // GLA_CHUNK
    assert blk % (2 * GLA_CHUNK) == 0 and n_sub <= SUBLANES
    fwd = lambda bi, i: (bi, i, 0)
    bwd = lambda bi, i: (bi, n_blocks - 1 - i, 0)
    row_specs = lambda f: [pl.BlockSpec((1, blk, nk), f), pl.BlockSpec((1, blk, nk), f),
                           pl.BlockSpec((1, blk, nv), f), pl.BlockSpec((1, blk, LANES), f)]
    state_spec = pl.BlockSpec((2, 1, GLA_HEADS, GLA_DK, GLA_DV), lambda bi, i: (0, bi, 0, 0, 0))
    return pl.pallas_call(
        functools.partial(_gla_scan_kernel, blk=blk, n_blocks=n_blocks),
        grid=(b, n_blocks),
        in_specs=row_specs(fwd) + row_specs(bwd) + [
            pl.BlockSpec((2, LANES, nk), lambda bi, i: (0, 0, 0)),
            pl.BlockSpec((2, 1, nk), lambda bi, i: (0, 0, 0)),
            state_spec,
        ],
        out_specs=[pl.BlockSpec((1, blk, nv), fwd), pl.BlockSpec((1, blk, nv), bwd), state_spec],
        out_shape=[jax.ShapeDtypeStruct((b, l, nv), F32), jax.ShapeDtypeStruct((b, l, nv), F32),
                   jax.ShapeDtypeStruct((2, b, GLA_HEADS, GLA_DK, GLA_DV), F32)],
        scratch_shapes=[pltpu.VMEM((2, GLA_HEADS, GLA_DV, GLA_DK), F32),
                        pltpu.VMEM((2, blk, nk), BF16), pltpu.VMEM((2, blk, nk), BF16),
                        pltpu.VMEM((2, blk, nk), BF16),
                        pltpu.VMEM((2, nv, blk), F32),
                        pltpu.VMEM((2, n_sub, GLA_HEADS, GLA_DV, GLA_DK), F32),
                        pltpu.VMEM((2, SUBLANES, nk), F32)],
        compiler_params=_cparams(("parallel", "arbitrary")),
        name="gla_scan",
    )(q, k, v, zl, q, k, v, zl, w2, b2, s0)


def _gla_out_kernel(of_ref, ob_ref, r_ref, gh_ref, w_ref, x_ref, mod_ref, out_ref):
    o = of_ref[0] + ob_ref[0]
    parts = []
    for h in range(GLA_HEADS):
        oh = o[:, h * GLA_DV:(h + 1) * GLA_DV]
        ms = jnp.mean(oh * oh, axis=-1, keepdims=True)
        parts.append((oh * lax.rsqrt(ms + EPS)) * gh_ref[...])
    y = jnp.concatenate(parts, axis=1) * _silu(r_ref[0].astype(F32))
    acc = jnp.dot(y.astype(BF16), w_ref[...], preferred_element_type=F32)
    out_ref[0] = x_ref[0] + mod_ref[0, 2:3, :] * acc


def _gla_out(o_f, o_b, r, g_head, w, x, mod, *, tm):
    b, l, nv = o_f.shape
    row = lambda bi, i: (bi, i, 0)
    return pl.pallas_call(
        _gla_out_kernel,
        grid=(b, l // tm),
        in_specs=[
            pl.BlockSpec((1, tm, nv), row),
            pl.BlockSpec((1, tm, nv), row),
            pl.BlockSpec((1, tm, nv), row),
            pl.BlockSpec((1, GLA_DV), lambda bi, i: (0, 0)),
            pl.BlockSpec((nv, D_MODEL), lambda bi, i: (0, 0)),
            pl.BlockSpec((1, tm, D_MODEL), row),
            _mod_spec(mod),
        ],
        out_specs=pl.BlockSpec((1, tm, D_MODEL), row),
        out_shape=jax.ShapeDtypeStruct((b, l, D_MODEL), F32),
        compiler_params=_cparams(("parallel", "parallel")),
        name="gla_out",
    )(o_f, o_b, r, g_head.reshape(1, GLA_DV), w, x, mod)


def _ssd_scan_kernel(x_ref, b_ref, c_ref, dt_ref, bias_ref, a_ref, dsk_ref, s0_ref, y_ref, sf_ref, s_scr, xt_scr,
                     *, reverse, n_chunks, lane_off, add_skip):
    i = pl.program_id(1)
    ch = SSD_CHUNK
    p = SSD_HEAD_DIM
    hpg = SSD_HEADS // SSD_GROUPS

    @pl.when(i == 0)
    def _():
        s_scr[...] = s0_ref[0]

    r = lax.broadcasted_iota(jnp.int32, (ch, ch), 0)
    c = lax.broadcasted_iota(jnp.int32, (ch, ch), 1)
    keep = (r <= c) if reverse else (r >= c)
    tri = keep.astype(BF16)
    edge = 0 if reverse else ch - 1

    dt = _softplus(dt_ref[0] + bias_ref[...])
    la = dt * a_ref[...]
    cum = _exact_dot_left01(tri, la) * LOG2_E
    cum_t = jnp.transpose(cum)
    dt_t = jnp.transpose(dt)
    to_end_t = jnp.transpose(jnp.exp2(cum[edge:edge + 1, :] - cum) * dt)
    xt_scr[...] = jnp.transpose(x_ref[0])
    keep_f = keep.astype(F32)
    lane = lax.broadcasted_iota(jnp.int32, (1, LANES), 1)
    lo_half = lane < p

    for g in range(SSD_GROUPS):
        bg = b_ref[0, :, g * SSD_STATE:(g + 1) * SSD_STATE].astype(BF16)
        cg = c_ref[0, :, g * SSD_STATE:(g + 1) * SSD_STATE].astype(BF16)
        cb = _dot_nt(cg, bg) * keep_f
        gs = slice(g * hpg * p, (g + 1) * hpg * p)
        s_g = s_scr[gs, :]
        y_inter = _dot_nt(cg, s_g.astype(BF16))
        for pair in range(hpg // 2):
            cols = slice((g * hpg + 2 * pair) * p, (g * hpg + 2 * pair + 2) * p)
            x_pair = x_ref[0, :, cols]
            ws = []
            e_is = []
            for t in range(2):
                hl = lane_off + g * hpg + 2 * pair + t
                cum_i = jnp.broadcast_to(cum[:, hl:hl + 1], (ch, ch))
                seg = jnp.minimum(cum_i - cum_t[hl:hl + 1, :], 0.0)
                ws.append((cb * jnp.exp2(seg) * dt_t[hl:hl + 1, :]).astype(BF16))
                e_is.append(jnp.exp2(cum_i))
            xb = x_pair.astype(BF16)
            zero = jnp.zeros_like(xb)
            x_bd = jnp.concatenate([jnp.where(lo_half, xb, zero), jnp.where(lo_half, zero, xb)], axis=0)
            y_pair = jnp.dot(jnp.concatenate(ws, axis=1), x_bd, preferred_element_type=F32)
            y_pair = y_pair + y_inter[:, 2 * pair * p:(2 * pair + 2) * p] * jnp.where(lo_half, e_is[0], e_is[1])
            if add_skip:
                hl0 = g * hpg + 2 * pair
                dsk = jnp.where(lo_half, dsk_ref[:, hl0:hl0 + 1], dsk_ref[:, hl0 + 1:hl0 + 2])
                y_pair = y_pair + x_pair * dsk
            y_ref[0, :, cols] = y_pair.astype(y_ref.dtype)
        xs_t = []
        for hh in range(hpg):
            hl = lane_off + g * hpg + hh
            rs = slice((g * hpg + hh) * p, (g * hpg + hh + 1) * p)
            xs_t.append((xt_scr[rs, :] * to_end_t[hl:hl + 1, :]).astype(BF16))
        ds = jnp.dot(jnp.concatenate(xs_t, axis=0), bg, preferred_element_type=F32)
        for hh in range(hpg):
            hl = lane_off + g * hpg + hh
            tot = jnp.exp2(cum_t[hl:hl + 1, edge:edge + 1])
            rs = slice((g * hpg + hh) * p, (g * hpg + hh + 1) * p)
            s_scr[rs, :] = s_scr[rs, :] * tot + ds[hh * p:(hh + 1) * p, :]

    @pl.when(i == n_chunks - 1)
    def _():
        sf_ref[0] = s_scr[...]


def _ssd_scan(x, bc, dt, bias, a, dskip, s0, *, reverse, add_skip):
    b, l, _ = x.shape
    ch = SSD_CHUNK
    n_chunks = l // ch
    gn = SSD_GROUPS * SSD_STATE
    rows = SSD_HEADS * SSD_HEAD_DIM

    def at(col):
        if reverse:
            return lambda bi, i: (bi, n_chunks - 1 - i, col)
        return lambda bi, i: (bi, i, col)

    vec = pl.BlockSpec((1, LANES), lambda bi, i: (0, 0))
    state_spec = pl.BlockSpec((1, rows, SSD_STATE), lambda bi, i: (bi, 0, 0))
    kern = functools.partial(_ssd_scan_kernel, reverse=reverse, n_chunks=n_chunks,
                             lane_off=SSD_HEADS if reverse else 0, add_skip=add_skip)
    return pl.pallas_call(
        kern,
        grid=(b, n_chunks),
        in_specs=[
            pl.BlockSpec((1, ch, SSD_D_INNER), at(0)),
            pl.BlockSpec((1, ch, gn), at(0)),
            pl.BlockSpec((1, ch, gn), at(1)),
            pl.BlockSpec((1, ch, LANES), at(0)),
            vec, vec, vec,
            state_spec,
        ],
        out_specs=[pl.BlockSpec((1, ch, SSD_D_INNER), at(0)), state_spec],
        out_shape=[jax.ShapeDtypeStruct((b, l, SSD_D_INNER), BF16),
                   jax.ShapeDtypeStruct((b, rows, SSD_STATE), F32)],
        scratch_shapes=[pltpu.VMEM((rows, SSD_STATE), F32), pltpu.VMEM((SSD_D_INNER, ch), F32)],
        compiler_params=_cparams(("parallel", "arbitrary")),
        name="ssd_scan_bwd" if reverse else "ssd_scan_fwd",
    )(x, bc, bc, dt, bias, a, dskip, s0)


def _ssd_out_kernel(yf_ref, yb_ref, z_ref, gn_ref, w_ref, x_ref, mod_ref, out_ref):
    y = (yf_ref[0].astype(F32) + yb_ref[0].astype(F32)) * _silu(z_ref[0].astype(F32))
    ms = jnp.mean(y * y, axis=-1, keepdims=True)
    y = (y * lax.rsqrt(ms + EPS)) * gn_ref[...]
    acc = jnp.dot(y.astype(BF16), w_ref[...], preferred_element_type=F32)
    out_ref[0] = x_ref[0] + mod_ref[0, 2:3, :] * acc


def _ssd_out(y_f, y_b, z, g_norm, w, x, mod, *, tm):
    b, l, di = y_f.shape
    row = lambda bi, i: (bi, i, 0)
    return pl.pallas_call(
        _ssd_out_kernel,
        grid=(b, l // tm),
        in_specs=[
            pl.BlockSpec((1, tm, di), row),
            pl.BlockSpec((1, tm, di), row),
            pl.BlockSpec((1, tm, di), row),
            pl.BlockSpec((1, di), lambda bi, i: (0, 0)),
            pl.BlockSpec((di, D_MODEL), lambda bi, i: (0, 0)),
            pl.BlockSpec((1, tm, D_MODEL), row),
            _mod_spec(mod),
        ],
        out_specs=pl.BlockSpec((1, tm, D_MODEL), row),
        out_shape=jax.ShapeDtypeStruct((b, l, D_MODEL), F32),
        compiler_params=_cparams(("parallel", "parallel")),
        name="ssd_out",
    )(y_f, y_b, z, g_norm.reshape(1, di), w, x, mod)


def _rope_tables(n_tokens, dim):
    rows = n_tokens // GRID_W
    row = jnp.repeat(jnp.arange(rows, dtype=F32), GRID_W)
    col = jnp.tile(jnp.arange(GRID_W, dtype=F32), rows)
    axis_dim = dim // 2
    inv = ROPE_BASE ** (-jnp.arange(0, axis_dim, 2, dtype=F32) / axis_dim)
    ar = row[:, None] * inv
    ac = col[:, None] * inv
    cos = jnp.concatenate([jnp.cos(ar), jnp.cos(ar), jnp.cos(ac), jnp.cos(ac)], axis=1)
    sin = jnp.concatenate([-jnp.sin(ar), jnp.sin(ar), -jnp.sin(ac), jnp.sin(ac)], axis=1)
    reps = LANES // dim
    return jnp.tile(cos, (1, reps)), jnp.tile(sin, (1, reps))


def _pad_cols(w, width):
    return jnp.pad(w, ((0, 0), (0, width - w.shape[1])))


def _conv_pack(conv_w, conv_b):
    return jnp.concatenate([conv_w, conv_b[None], jnp.zeros((4, conv_w.shape[1]), F32)], axis=0)


def _tm_for(l):
    return min(l, 512)


def _key_chunk(n_keys):
    for kc in (2304, 1536, 768, 512, 256):
        if n_keys % kc == 0:
            return kc
    return n_keys


def kernel(x_prompt, x_sample, c, c_ctx, cache_win_k, cache_win_v, state_gla_fwd, state_gla_bwd, cache_diff_k, cache_diff_v, state_ssd_fwd, state_ssd_bwd, ada_w, ada_b, norm_mix, norm_ffn, ffn_w_up, ffn_conv_w, ffn_conv_b, ffn_w_down, final_norm, win_w_qkv, win_w_o, win_sink, gla_w_qkvr, gla_w_gf1, gla_w_gf2, gla_b_gf, gla_w_gb1, gla_w_gb2, gla_b_gb, gla_norm, gla_w_o, diff_w_qkv, diff_lq1, diff_lk1, diff_lq2, diff_lk2, diff_norm, diff_w_o, ssd_w_in, ssd_conv_w, ssd_conv_b, ssd_a_log_f, ssd_a_log_b, ssd_dt_bias_f, ssd_dt_bias_b, ssd_d, ssd_norm, ssd_w_out):
    xp, xs = x_prompt, x_sample
    bp, lp, d = xp.shape
    bs, ls, _ = xs.shape
    tmp, tms = _tm_for(lp), _tm_for(ls)

    n_cond = 1 + bs
    cond_rows = -(-n_cond // SUBLANES) * SUBLANES
    cond = jnp.concatenate([c_ctx[None], c, jnp.zeros((cond_rows - n_cond, d), F32)], axis=0)
    mods = _ada_call(cond, ada_w, ada_b).reshape(DEPTH, cond_rows, 6, d)
    mods = jnp.pad(mods, ((0, 0), (0, 0), (0, 2), (0, 0)))

    outs = {}
    for i in range(DEPTH):
        kind, j = i % 4, i // 4
        mod_p = mods[i, 0:1]
        mod_s = mods[i, 1:1 + bs]
        if kind == 0:
            nq = WA_HEADS * WA_HEAD_DIM
            nkv = WA_KV_HEADS * WA_HEAD_DIM
            perm = jnp.arange(nq).reshape(WA_KV_HEADS, WA_GROUP, WA_HEAD_DIM).transpose(1, 0, 2).reshape(-1)
            wq = win_w_qkv[j][:, :nq][:, perm]
            w = jnp.concatenate([wq, win_w_qkv[j][:, nq:]], axis=1).astype(BF16)
            w_o = win_w_o[j][perm, :].astype(BF16)
            sink = win_sink[j]
            qscale = WA_HEAD_DIM ** -0.5 * LOG2_E
            q, k, v = _fused_proj(xp, norm_mix[i], mod_p, w,
                                  [(nq, "plain", qscale), (nkv, "plain", 1.0), (nkv, "plain", 1.0)],
                                  [BF16, F32, F32], tm=tmp, name="win_proj_ctx")
            outs["win_k"] = k.reshape(bp, 1, lp, WA_KV_HEADS, WA_HEAD_DIM)
            outs["win_v"] = v.reshape(bp, 1, lp, WA_KV_HEADS, WA_HEAD_DIM)
            o = _win_ctx_attn(q, k, v, sink, tq=min(lp, 128))
            xp = _out_proj(o, w_o, xp, mod_p, tm=tmp, name="win_out_ctx")
            rope = _rope_tables(ls, WA_HEAD_DIM)
            q, k, v = _fused_proj(xs, norm_mix[i], mod_s, w,
                                  [(nq, "rope", qscale), (nkv, "rope", 1.0), (nkv, "plain", 1.0)],
                                  [BF16, BF16, BF16], tm=tms, rope=rope, name="win_proj_lat")
            n_ctx = cache_win_k.shape[2]
            kctx = cache_win_k[:, j].reshape(bs, n_ctx, nkv).astype(BF16)
            vctx = cache_win_v[:, j].reshape(bs, n_ctx, nkv).astype(BF16)
            o = _win_lat_attn(q, k, v, kctx, vctx, sink, tq=min(ls, WINDOW))
            xs = _out_proj(o, w_o, xs, mod_s, tm=tms, name="win_out_lat")
        elif kind == 1:
            nk = GLA_HEADS * GLA_DK
            nv = GLA_HEADS * GLA_DV
            w1 = _pad_cols(jnp.concatenate([gla_w_gf1[j], gla_w_gb1[j]], axis=1), LANES)
            w = jnp.concatenate([gla_w_qkvr[j], w1], axis=1).astype(BF16)
            segs = [(nk, "plain", 1.0), (nk, "plain", 1.0), (nv, "plain", 1.0), (nv, "plain", 1.0),
                    (LANES, "plain", 1.0)]
            zrows = jnp.zeros((LANES - 2 * GLA_RANK, nk), F32)
            w2_f = jnp.concatenate([gla_w_gf2[j], jnp.zeros((GLA_RANK, nk), F32), zrows], axis=0).astype(BF16)
            w2_b = jnp.concatenate([jnp.zeros((GLA_RANK, nk), F32), gla_w_gb2[j], zrows], axis=0).astype(BF16)
            w2 = jnp.stack([w2_f, w2_b], axis=0)
            b2 = jnp.stack([gla_b_gf[j], gla_b_gb[j]], axis=0).reshape(2, 1, nk)
            w_o = gla_w_o[j].astype(BF16)
            dtypes = [F32, F32, BF16, BF16, F32]
            for stream in ("p", "s"):
                if stream == "p":
                    x, mod, tm = xp, mod_p, tmp
                    s0 = jnp.zeros((2, bp, GLA_HEADS, GLA_DK, GLA_DV), F32)
                else:
                    x, mod, tm = xs, mod_s, tms
                    s0 = jnp.stack([state_gla_fwd[:, j], state_gla_bwd[:, j]], axis=0)
                q, k, v, r, zl = _fused_proj(x, norm_mix[i], mod, w, segs, dtypes, tm=tm,
                                             name="gla_proj_" + stream)
                o_f, o_b, s_fb = _gla_scan(q, k, v, zl, w2, b2, s0, blk=min(x.shape[1], 512))
                x = _gla_out(o_f, o_b, r, gla_norm[j], w_o, x, mod, tm=tm)
                if stream == "p":
                    xp = x
                    outs["gla_f"] = s_fb[0][:, None]
                    outs["gla_b"] = s_fb[1][:, None]
                else:
                    xs = x
        elif kind == 2:
            lam_init = 0.8 - 0.6 * math.exp(-0.3 * i)
            nh = DA_HEADS * 2 * DA_HEAD_DIM
            w = diff_w_qkv[j].astype(BF16)
            w_o = diff_w_o[j].astype(BF16)
            qscale = DA_HEAD_DIM ** -0.5 * LOG2_E
            lqk = jnp.stack([diff_lq1[j], diff_lk1[j], diff_lq2[j], diff_lk2[j]], axis=0)
            lqk = jnp.pad(lqk, ((0, 4), (0, 2 * DA_HEAD_DIM - lqk.shape[1])))
            gsub = diff_norm[j].reshape(1, 2 * DA_HEAD_DIM)
            q, k, v = _fused_proj(xp, norm_mix[i], mod_p, w,
                                  [(nh, "plain", qscale), (nh, "plain", 1.0), (nh, "plain", 1.0)],
                                  [BF16, F32, F32], tm=tmp, name="diff_proj_ctx")
            outs["diff_k"] = k.reshape(bp, 1, lp, DA_HEADS, 2, DA_HEAD_DIM)
            outs["diff_v"] = v.reshape(bp, 1, lp, DA_HEADS, 2 * DA_HEAD_DIM)
            o = _diff_attn(q, k, v, None, None, lqk, gsub, tq=min(lp, 256), kc=_key_chunk(lp), lam_init=lam_init)
            xp = _out_proj(o, w_o, xp, mod_p, tm=tmp, name="diff_out_ctx")
            rope = _rope_tables(ls, DA_HEAD_DIM)
            q, k, v = _fused_proj(xs, norm_mix[i], mod_s, w,
                                  [(nh, "rope", qscale), (nh, "rope", 1.0), (nh, "plain", 1.0)],
                                  [BF16, BF16, BF16], tm=tms, rope=rope, name="diff_proj_lat")
            n_ctx = cache_diff_k.shape[2]
            kctx = cache_diff_k[:, j].reshape(bs, n_ctx, nh).astype(BF16)
            vctx = cache_diff_v[:, j].reshape(bs, n_ctx, nh).astype(BF16)
            o = _diff_attn(q, k, v, kctx, vctx, lqk, gsub, tq=min(ls, 512), kc=_key_chunk(ls + n_ctx),
                           lam_init=lam_init)
            xs = _out_proj(o, w_o, xs, mod_s, tm=tms, name="diff_out_lat")
        else:
            gn = SSD_GROUPS * SSD_STATE
            nxbc = SSD_D_INNER + 2 * gn
            w_in = ssd_w_in[j]
            w = jnp.concatenate([w_in[:, :SSD_D_INNER + nxbc], _pad_cols(w_in[:, SSD_D_INNER + nxbc:], LANES)],
                                axis=1).astype(BF16)
            segs = [(SSD_D_INNER, "plain", 1.0), (SSD_D_INNER, "conv_silu", 1.0), (2 * gn, "conv_silu", 1.0),
                    (LANES, "plain", 1.0)]
            seg_dtypes = [BF16, F32, BF16, F32]
            cw = _conv_pack(ssd_conv_w[j], ssd_conv_b[j])
            zpad = jnp.zeros((LANES - 2 * SSD_HEADS,), F32)
            zh = jnp.zeros((SSD_HEADS,), F32)
            bias = jnp.concatenate([ssd_dt_bias_f[j], ssd_dt_bias_b[j], zpad]).reshape(1, LANES)
            a_f = jnp.concatenate([-jnp.exp(ssd_a_log_f[j]), zh, zpad]).reshape(1, LANES)
            a_b = jnp.concatenate([zh, -jnp.exp(ssd_a_log_b[j]), zpad]).reshape(1, LANES)
            dsk = jnp.concatenate([ssd_d[j], zh, zpad]).reshape(1, LANES)
            w_out = ssd_w_out[j].astype(BF16)
            rows = SSD_HEADS * SSD_HEAD_DIM
            for stream in ("p", "s"):
                if stream == "p":
                    x, mod, tm, bsz = xp, mod_p, tmp, bp
                    s0_f = jnp.zeros((bp, rows, SSD_STATE), F32)
                    s0_b = s0_f
                else:
                    x, mod, tm, bsz = xs, mod_s, tms, bs
                    s0_f = state_ssd_fwd[:, j].reshape(bs, rows, SSD_STATE)
                    s0_b = state_ssd_bwd[:, j].reshape(bs, rows, SSD_STATE)
                z, xc, bc, dt = _fused_proj(x, norm_mix[i], mod, w, segs, seg_dtypes, tm=tm, conv_w=cw,
                                            name="ssd_proj_" + stream)
                y_f, s_f = _ssd_scan(xc, bc, dt, bias, a_f, dsk, s0_f, reverse=False, add_skip=True)
                y_b, s_b = _ssd_scan(xc, bc, dt, bias, a_b, dsk, s0_b, reverse=True, add_skip=False)
                x = _ssd_out(y_f, y_b, z, ssd_norm[j], w_out, x, mod, tm=tm)
                if stream == "p":
                    xp = x
                    outs["ssd_f"] = s_f.reshape(bp, 1, SSD_HEADS, SSD_HEAD_DIM, SSD_STATE)
                    outs["ssd_b"] = s_b.reshape(bp, 1, SSD_HEADS, SSD_HEAD_DIM, SSD_STATE)
                else:
                    xs = x

        w_up = ffn_w_up[i].astype(BF16)
        cw = _conv_pack(ffn_conv_w[i], ffn_conv_b[i])
        wd = ffn_w_down[i].astype(BF16)
        last = i == DEPTH - 1
        xp = _ffn(xp, norm_ffn[i], mod_p, w_up, cw, wd, final_norm, tm=tmp, final_norm=last, name="ffn_p")
        xs = _ffn(xs, norm_ffn[i], mod_s, w_up, cw, wd, final_norm, tm=min(ls, 2 * tms), final_norm=last,
                  name="ffn_s")

    return (xp, xs, outs["win_k"], outs["win_v"], outs["gla_f"], outs["gla_b"],
            outs["diff_k"], outs["diff_v"], outs["ssd_f"], outs["ssd_b"])
```
